```python
import math
import jax, jax.numpy as jnp
from jax import lax
import numpy as np

D_MODEL = 1024
BATCH = 8
SEQ = 4096
DEPTH = 1

RET_HEADS = 4
RET_QK_DIM = 256
RET_V_DIM = 512
RET_QK = RET_HEADS * RET_QK_DIM
RET_V = RET_HEADS * RET_V_DIM
RET_CHUNK = 128
ROPE_BASE = 10000.0
SSM_INNER = 2 * D_MODEL
SSM_HEAD_DIM = 64
SSM_HEADS = SSM_INNER // SSM_HEAD_DIM
SSM_GROUPS = 4
SSM_HPG = SSM_HEADS // SSM_GROUPS
SSM_STATE = 128
SSM_CONV = 5
SSM_CHUNK = 128
SSM_XBC = SSM_INNER + 2 * SSM_GROUPS * SSM_STATE
D_FF = 4 * D_MODEL
N_BRANCH = 2
EPS = 1e-6
IN_SIZES = (RET_QK, RET_QK, RET_V, RET_V, SSM_INNER, SSM_XBC, 2 * SSM_HEADS, N_BRANCH * D_MODEL)
D_IN = RET_QK * 2 + RET_V * 2 + SSM_INNER + SSM_XBC + 2 * SSM_HEADS + N_BRANCH * D_MODEL

kernel_name = 'hybrid_retention_ssd_gated_encoder'


def _split(t, sizes):
    offs = np.cumsum(np.array(sizes))[:-1].tolist()
    return jnp.split(t, offs, axis=-1)


def _rmsnorm(x, g):
    xf = x.astype(jnp.float32)
    y = xf * lax.rsqrt(jnp.mean(xf * xf, axis=-1, keepdims=True) + EPS)
    return (y * g.astype(jnp.float32)).astype(x.dtype)


def _rope(t, positions):
    half = t.shape[-1] // 2
    inv = ROPE_BASE ** (-jnp.arange(half, dtype=jnp.float32) / half)
    ang = positions.astype(jnp.float32)[..., None] * inv
    cos = jnp.cos(ang)[:, :, None, :]
    sin = jnp.sin(ang)[:, :, None, :]
    t1, t2 = t[..., :half], t[..., half:]
    return jnp.concatenate([t1 * cos - t2 * sin, t1 * sin + t2 * cos], axis=-1)


def _ret_cross(qc, kc, vc, log_g, reverse):
    pos = jnp.arange(RET_CHUNK, dtype=jnp.float32)
    if reverse:
        q_exp, k_exp = RET_CHUNK - pos, pos
    else:
        q_exp, k_exp = pos + 1.0, RET_CHUNK - 1.0 - pos
    q_dec = jnp.exp(q_exp[None, :] * log_g[:, None])[:, :, None]
    k_dec = jnp.exp(k_exp[None, :] * log_g[:, None])[:, :, None]
    chunk_dec = jnp.exp(RET_CHUNK * log_g)[:, None, None]
    qs, ks = qc * q_dec, kc * k_dec

    def step(state, inp):
        q_i, k_i, v_i = inp
        out = jnp.einsum('bhck,bhkv->bhcv', q_i, state)
        state = state * chunk_dec + jnp.einsum('bhck,bhcv->bhkv', k_i, v_i)
        return state, out

    init = jnp.zeros((qc.shape[1], RET_HEADS, RET_QK_DIM, RET_V_DIM), jnp.float32)
    _, out = lax.scan(step, init, (qs, ks, vc), reverse=reverse)
    return out


def _retention(q, k, v, g, positions, gn_g):
    f32 = jnp.float32
    bsz, seq = q.shape[0], q.shape[1]
    nc = seq // RET_CHUNK
    q = _rope(q.astype(f32).reshape(bsz, seq, RET_HEADS, RET_QK_DIM), positions)
    k = _rope(k.astype(f32).reshape(bsz, seq, RET_HEADS, RET_QK_DIM), positions) * (RET_QK_DIM ** -0.5)
    v = v.astype(f32).reshape(bsz, seq, RET_HEADS, RET_V_DIM)
    log_g = jnp.log1p(-jnp.exp2(-5.0 - jnp.arange(RET_HEADS, dtype=f32)))

    def to_chunks(t):
        return t.reshape(bsz, nc, RET_CHUNK, RET_HEADS, t.shape[-1]).transpose(1, 0, 3, 2, 4)

    qc, kc, vc = to_chunks(q), to_chunks(k), to_chunks(v)
    idx = jnp.arange(RET_CHUNK, dtype=f32)
    dist = jnp.abs(idx[:, None] - idx[None, :])
    intra_dec = jnp.exp(dist[None] * log_g[:, None, None])
    scores = jnp.einsum('nbhik,nbhjk->nbhij', qc, kc) * intra_dec
    y = jnp.einsum('nbhij,nbhjv->nbhiv', scores, vc)
    y = y + _ret_cross(qc, kc, vc, log_g, False) + _ret_cross(qc, kc, vc, log_g, True)
    y = y.transpose(1, 0, 3, 2, 4).reshape(bsz, seq, RET_HEADS, RET_V_DIM)
    mu = jnp.mean(y, axis=-1, keepdims=True)
    var = jnp.mean(jnp.square(y - mu), axis=-1, keepdims=True)
    y = ((y - mu) * lax.rsqrt(var + EPS)).reshape(bsz, seq, RET_V) * gn_g.astype(f32)
    return (y * jax.nn.silu(g.astype(f32))).astype(g.dtype)


def _ssd_scan(xdt, la, bm, cm):
    bsz, seq = xdt.shape[0], xdt.shape[1]
    nc = seq // SSM_CHUNK

    def to_chunks(t):
        return jnp.moveaxis(t.reshape((bsz, nc, SSM_CHUNK) + t.shape[2:]), 1, 0)

    causal = jnp.tril(jnp.ones((SSM_CHUNK, SSM_CHUNK), dtype=bool))

    def step(state, inp):
        xc, lac, bc, cc = inp
        cum = jnp.cumsum(lac, axis=1)
        seg = cum[:, :, None, :] - cum[:, None, :, :]
        L = jnp.exp(jnp.where(causal[None, :, :, None], seg, -jnp.inf))
        L = L.reshape(bsz, SSM_CHUNK, SSM_CHUNK, SSM_GROUPS, SSM_HPG)
        xg = xc.reshape(bsz, SSM_CHUNK, SSM_GROUPS, SSM_HPG, SSM_HEAD_DIM)
        cb = jnp.einsum('bign,bjgn->bgij', cc, bc)
        y_in = jnp.einsum('bgij,bijgh,bjghp->bighp', cb, L, xg)
        dec_in = jnp.exp(cum).reshape(bsz, SSM_CHUNK, SSM_GROUPS, SSM_HPG, 1)
        y_st = jnp.einsum('bign,bghpn->bighp', cc, state) * dec_in
        dec_out = jnp.exp(cum[:, -1:, :] - cum).reshape(bsz, SSM_CHUNK, SSM_GROUPS, SSM_HPG)
        dec_chunk = jnp.exp(cum[:, -1, :]).reshape(bsz, SSM_GROUPS, SSM_HPG, 1, 1)
        state = state * dec_chunk + jnp.einsum('bjgn,bjgh,bjghp->bghpn', bc, dec_out, xg)
        return state, (y_in + y_st).reshape(bsz, SSM_CHUNK, SSM_HEADS, SSM_HEAD_DIM)

    init = jnp.zeros((bsz, SSM_GROUPS, SSM_HPG, SSM_HEAD_DIM, SSM_STATE), jnp.float32)
    _, y = lax.scan(step, init, (to_chunks(xdt), to_chunks(la), to_chunks(bm), to_chunks(cm)))
    return jnp.moveaxis(y, 0, 1).reshape(bsz, seq, SSM_HEADS, SSM_HEAD_DIM)


def _ssd_mixer(z, xbc, dt_raw, conv_w, conv_b, dt_bias_f, dt_bias_b, a_log_f, a_log_b, d_skip, norm_g):
    f32 = jnp.float32
    bsz, seq = z.shape[0], z.shape[1]
    xbc = lax.conv_general_dilated(
        xbc, conv_w[:, None, :].astype(xbc.dtype), window_strides=(1,),
        padding=((SSM_CONV // 2, SSM_CONV // 2),),
        dimension_numbers=('NWC', 'WIO', 'NWC'), feature_group_count=SSM_XBC)
    xbc = jax.nn.silu((xbc + conv_b).astype(f32))
    xs, bm, cm = _split(xbc, (SSM_INNER, SSM_GROUPS * SSM_STATE, SSM_GROUPS * SSM_STATE))
    xh = xs.reshape(bsz, seq, SSM_HEADS, SSM_HEAD_DIM)
    bm = bm.reshape(bsz, seq, SSM_GROUPS, SSM_STATE)
    cm = cm.reshape(bsz, seq, SSM_GROUPS, SSM_STATE)
    dt_f_raw, dt_b_raw = _split(dt_raw.astype(f32), (SSM_HEADS, SSM_HEADS))
    dt_f = jax.nn.softplus(dt_f_raw + dt_bias_f.astype(f32))
    dt_b = jax.nn.softplus(dt_b_raw + dt_bias_b.astype(f32))
    a_f = -jnp.exp(a_log_f.astype(f32))
    a_b = -jnp.exp(a_log_b.astype(f32))
    y_f = _ssd_scan(xh * dt_f[..., None], dt_f * a_f, bm, cm)

    def flip(t):
        return jnp.flip(t, axis=1)

    y_b = flip(_ssd_scan(flip(xh * dt_b[..., None]), flip(dt_b * a_b), flip(bm), flip(cm)))
    y = y_f + y_b + d_skip.astype(f32)[:, None] * xh
    y = y.reshape(bsz, seq, SSM_INNER) * jax.nn.silu(z.astype(f32))
    yg = y.reshape(bsz, seq, SSM_GROUPS, SSM_INNER // SSM_GROUPS)
    yg = yg * lax.rsqrt(jnp.mean(yg * yg, axis=-1, keepdims=True) + EPS)
    return (yg.reshape(bsz, seq, SSM_INNER) * norm_g.astype(f32)).astype(z.dtype)


def setup_inputs(seed: int = 0) -> dict:
    key = jax.random.key(seed)
    ks = jax.random.split(key, 20)
    nrm = jax.random.normal

    def gain(k, n):
        return 1.0 + 0.02 * nrm(k, (DEPTH, n), jnp.float32)

    def dt_bias(k):
        u = jax.random.uniform(k, (DEPTH, SSM_HEADS), jnp.float32)
        dt = jnp.exp(u * (math.log(0.1) - math.log(1e-3)) + math.log(1e-3))
        return dt + jnp.log(-jnp.expm1(-dt))

    def a_log(k):
        return jnp.log(jax.random.uniform(k, (DEPTH, SSM_HEADS), jnp.float32, 1.0, 16.0))

    offset = jax.random.randint(ks[1], (BATCH, 1), 0, 1024, dtype=jnp.int32)
    positions = jnp.arange(SEQ, dtype=jnp.int32)[None, :] + offset
    return {
        'x': nrm(ks[0], (BATCH, SEQ, D_MODEL), jnp.float32),
        'positions': positions,
        'norm_mix_g': gain(ks[2], D_MODEL),
        'w_in': nrm(ks[3], (DEPTH, D_MODEL, D_IN), jnp.float32) * D_MODEL ** -0.5,
        'ret_gn_g': gain(ks[4], RET_V),
        'w_ret_o': nrm(ks[5], (DEPTH, RET_V, D_MODEL), jnp.float32) * RET_V ** -0.5,
        'conv_w': nrm(ks[6], (DEPTH, SSM_CONV, SSM_XBC), jnp.float32) * SSM_CONV ** -0.5,
        'conv_b': 0.02 * nrm(ks[7], (DEPTH, SSM_XBC), jnp.float32),
        'dt_bias_f': dt_bias(ks[8]),
        'dt_bias_b': dt_bias(ks[9]),
        'a_log_f': a_log(ks[10]),
        'a_log_b': a_log(ks[11]),
        'ssm_d': 1.0 + 0.02 * nrm(ks[12], (DEPTH, SSM_HEADS), jnp.float32),
        'ssm_norm_g': gain(ks[13], SSM_INNER),
        'w_ssm_o': nrm(ks[14], (DEPTH, SSM_INNER, D_MODEL), jnp.float32) * SSM_INNER ** -0.5,
        'w_out': nrm(ks[15], (DEPTH, D_MODEL, D_MODEL), jnp.float32) * D_MODEL ** -0.5,
        'norm_mlp_g': gain(ks[16], D_MODEL),
        'w_mlp_up': nrm(ks[17], (DEPTH, D_MODEL, D_FF), jnp.float32) * D_MODEL ** -0.5,
        'w_mlp_down': nrm(ks[18], (DEPTH, D_FF, D_MODEL), jnp.float32) * D_FF ** -0.5,
        'norm_final_g': 1.0 + 0.02 * nrm(ks[19], (D_MODEL,), jnp.float32),
    }


def reference(x, positions, norm_mix_g, w_in, ret_gn_g, w_ret_o, conv_w, conv_b, dt_bias_f, dt_bias_b,
              a_log_f, a_log_b, ssm_d, ssm_norm_g, w_ssm_o, w_out, norm_mlp_g, w_mlp_up, w_mlp_down,
              norm_final_g):
    for l in range(DEPTH):
        h = _rmsnorm(x, norm_mix_g[l])
        proj = jnp.einsum('bsd,de->bse', h, w_in[l])
        q, k, v, g, z, xbc, dt_raw, gates = _split(proj, IN_SIZES)
        y_ret = jnp.einsum('bse,ed->bsd', _retention(q, k, v, g, positions, ret_gn_g[l]), w_ret_o[l])
        y_ssm = jnp.einsum('bse,ed->bsd', _ssd_mixer(z, xbc, dt_raw, conv_w[l], conv_b[l], dt_bias_f[l],
                                                     dt_bias_b[l], a_log_f[l], a_log_b[l], ssm_d[l],
                                                     ssm_norm_g[l]), w_ssm_o[l])
        gate_ret, gate_ssm = _split(gates, (D_MODEL, D_MODEL))
        mixed = jax.nn.sigmoid(gate_ret) * y_ret + jax.nn.sigmoid(gate_ssm) * y_ssm
        x = x + jnp.einsum('bsd,de->bse', mixed, w_out[l])
        h = _rmsnorm(x, norm_mlp_g[l])
        up = jnp.square(jax.nn.relu(jnp.einsum('bsd,df->bsf', h, w_mlp_up[l])))
        x = x + jnp.einsum('bsf,fd->bsd', up, w_mlp_down[l])
    return _rmsnorm(x, norm_final_g)
```

```python
import functools

import jax
import jax.numpy as jnp
from jax import lax
from jax.experimental import pallas as pl
from jax.experimental.pallas import tpu as pltpu

F32 = jnp.float32
BF16 = jnp.bfloat16

D_MODEL = 1024
RET_HEADS = 4
RET_QK_DIM = 256
RET_V_DIM = 512
RET_QK = RET_HEADS * RET_QK_DIM
RET_V = RET_HEADS * RET_V_DIM
ROPE_BASE = 10000.0
ROPE_HALF = RET_QK_DIM // 2
SSM_INNER = 2 * D_MODEL
SSM_HEAD_DIM = 64
SSM_HEADS = SSM_INNER // SSM_HEAD_DIM
SSM_GROUPS = 4
SSM_HPG = SSM_HEADS // SSM_GROUPS
SSM_STATE = 128
SSM_CONV = 5
SSM_GROUP_W = SSM_HPG * SSM_HEAD_DIM
D_FF = 4 * D_MODEL
EPS = 1e-6
CHUNK = 128

LANES = 128
CONV_HALO = 8

COL_Q = 0
COL_K = COL_Q + RET_QK
COL_V = COL_K + RET_QK
COL_G = COL_V + RET_V
COL_Z = COL_G + RET_V
COL_X = COL_Z + SSM_INNER
COL_B = COL_X + SSM_INNER
COL_C = COL_B + SSM_GROUPS * SSM_STATE
COL_GATE_RET = COL_C + SSM_GROUPS * SSM_STATE
COL_GATE_SSM = COL_GATE_RET + D_MODEL
PROJ_W = COL_GATE_SSM + D_MODEL

VMEM_LIMIT = 56 * 1024 * 1024


def _cparams(sem):
    return pltpu.CompilerParams(dimension_semantics=sem, vmem_limit_bytes=VMEM_LIMIT)


def _inproj_kernel(x_ref, pos_ref, inv_ref, g_ref, w_ref, wdt_ref, proj_ref, dt_ref,
                   h_sc, cos_sc, sin_sc):
    j = pl.program_id(1)

    @pl.when(j == 0)
    def _():
        x = x_ref[...]
        ms = jnp.mean(x * x, axis=-1, keepdims=True)
        h = ((x * lax.rsqrt(ms + EPS)) * g_ref[...]).astype(BF16)
        h_sc[...] = h
        dt_ref[...] = jnp.dot(h, wdt_ref[...], preferred_element_type=F32)
        ang = pos_ref[...].astype(F32) * inv_ref[...]
        cos_sc[...] = jnp.cos(ang)
        sin_sc[...] = jnp.sin(ang)

    acc = jnp.dot(h_sc[...], w_ref[...], preferred_element_type=F32)

    @pl.when(j < 2)
    def _():
        scale = jnp.where(j == 0, 1.0, RET_QK_DIM ** -0.5).astype(F32)
        c = cos_sc[...]
        s = sin_sc[...]
        for hd in range(RET_HEADS):
            lo = hd * RET_QK_DIM
            t1 = acc[:, lo:lo + ROPE_HALF]
            t2 = acc[:, lo + ROPE_HALF:lo + RET_QK_DIM]
            proj_ref[:, lo:lo + ROPE_HALF] = ((t1 * c - t2 * s) * scale).astype(BF16)
            proj_ref[:, lo + ROPE_HALF:lo + RET_QK_DIM] = ((t1 * s + t2 * c) * scale).astype(BF16)

    @pl.when(j >= 2)
    def _():
        proj_ref[...] = acc.astype(BF16)


def _inproj(x2, pos2, inv, g, w_main, w_dt, tm):
    m = x2.shape[0]
    tn = RET_QK
    grid = (m // tm, PROJ_W // tn)
    return pl.pallas_call(
        _inproj_kernel,
        grid=grid,
        in_specs=[
            pl.BlockSpec((tm, D_MODEL), lambda i, j: (i, 0)),
            pl.BlockSpec((tm, 1), lambda i, j: (i, 0)),
            pl.BlockSpec((1, ROPE_HALF), lambda i, j: (0, 0)),
            pl.BlockSpec((1, D_MODEL), lambda i, j: (0, 0)),
            pl.BlockSpec((D_MODEL, tn), lambda i, j: (0, j)),
            pl.BlockSpec((D_MODEL, SSM_GROUPS * LANES), lambda i, j: (0, 0)),
        ],
        out_specs=[
            pl.BlockSpec((tm, tn), lambda i, j: (i, j)),
            pl.BlockSpec((tm, SSM_GROUPS * LANES), lambda i, j: (i, 0)),
        ],
        out_shape=[
            jax.ShapeDtypeStruct((m, PROJ_W), BF16),
            jax.ShapeDtypeStruct((m, SSM_GROUPS * LANES), F32),
        ],
        scratch_shapes=[
            pltpu.VMEM((tm, D_MODEL), BF16),
            pltpu.VMEM((tm, ROPE_HALF), F32),
            pltpu.VMEM((tm, ROPE_HALF), F32),
        ],
        compiler_params=_cparams(("parallel", "arbitrary")),
        name="inproj",
    )(x2, pos2, inv, g, w_main, w_dt)


def _ret_kernel(logg_ref, q_ref, k_ref, v_ref, g_ref, gn_ref, o_ref, sb_sc, st_sc, *, nc):
    lg = logg_ref[pl.program_id(1)]
    pos = lax.broadcasted_iota(jnp.int32, (CHUNK, RET_QK_DIM), 0).astype(F32)
    qdec_f = jnp.exp((pos + 1.0) * lg)
    kdec_f = jnp.exp((CHUNK - 1.0 - pos) * lg)
    qdec_b = jnp.exp((CHUNK - pos) * lg)
    kdec_b = jnp.exp(pos * lg)
    chunk_dec = jnp.exp(jnp.full((1, RET_V_DIM), float(CHUNK), F32) * lg)
    ii = lax.broadcasted_iota(jnp.int32, (CHUNK, CHUNK), 0)
    jj = lax.broadcasted_iota(jnp.int32, (CHUNK, CHUNK), 1)
    intra_dec = jnp.exp(jnp.abs(ii - jj).astype(F32) * lg)
    tdims = (((0,), (0,)), ((), ()))

    st_sc[...] = jnp.zeros_like(st_sc)

    def sweep_back(t, carry):
        c = nc - 1 - t
        rows = pl.ds(pl.multiple_of(c * CHUNK, CHUNK), CHUNK)
        sb_sc[c] = st_sc[...].astype(BF16)
        kb = (k_ref[rows, :].astype(F32) * kdec_b).astype(BF16)
        upd = lax.dot_general(kb, v_ref[rows, :], tdims, preferred_element_type=F32)
        st_sc[...] = st_sc[...] * chunk_dec + upd
        return carry

    lax.fori_loop(0, nc, sweep_back, 0)

    st_sc[...] = jnp.zeros_like(st_sc)

    def sweep_fwd(c, carry):
        rows = pl.ds(pl.multiple_of(c * CHUNK, CHUNK), CHUNK)
        qb16 = q_ref[rows, :]
        kb16 = k_ref[rows, :]
        vc = v_ref[rows, :]
        qf32 = qb16.astype(F32)
        scores = lax.dot_general(qb16, kb16, (((1,), (1,)), ((), ())), preferred_element_type=F32)
        y = jnp.dot((scores * intra_dec).astype(BF16), vc, preferred_element_type=F32)
        y = y + jnp.dot((qf32 * qdec_f).astype(BF16), st_sc[...].astype(BF16),
                        preferred_element_type=F32)
        y = y + jnp.dot((qf32 * qdec_b).astype(BF16), sb_sc[c], preferred_element_type=F32)
        mu = jnp.mean(y, axis=-1, keepdims=True)
        d = y - mu
        var = jnp.mean(d * d, axis=-1, keepdims=True)
        yn = (d * lax.rsqrt(var + EPS)) * gn_ref[...]
        gate = g_ref[rows, :].astype(F32)
        o_ref[rows, :] = (yn * jax.nn.silu(gate)).astype(BF16)
        kf = (kb16.astype(F32) * kdec_f).astype(BF16)
        upd = lax.dot_general(kf, vc, tdims, preferred_element_type=F32)
        st_sc[...] = st_sc[...] * chunk_dec + upd
        return carry

    lax.fori_loop(0, nc, sweep_fwd, 0)


def _retention(proj, logg, gn_g, bsz, seq):
    nc = seq // CHUNK
    qb = COL_Q // RET_QK_DIM
    kb = COL_K // RET_QK_DIM
    vb = COL_V // RET_V_DIM
    gb = COL_G // RET_V_DIM
    return pl.pallas_call(
        functools.partial(_ret_kernel, nc=nc),
        grid=(bsz, RET_HEADS),
        in_specs=[
            pl.BlockSpec(memory_space=pltpu.SMEM),
            pl.BlockSpec((seq, RET_QK_DIM), lambda b, h: (b, qb + h)),
            pl.BlockSpec((seq, RET_QK_DIM), lambda b, h: (b, kb + h)),
            pl.BlockSpec((seq, RET_V_DIM), lambda b, h: (b, vb + h)),
            pl.BlockSpec((seq, RET_V_DIM), lambda b, h: (b, gb + h)),
            pl.BlockSpec((1, RET_V_DIM), lambda b, h: (0, h)),
        ],
        out_specs=pl.BlockSpec((seq, RET_V_DIM), lambda b, h: (b, h)),
        out_shape=jax.ShapeDtypeStruct((bsz * seq, RET_V), BF16),
        scratch_shapes=[
            pltpu.VMEM((nc, RET_QK_DIM, RET_V_DIM), BF16),
            pltpu.VMEM((RET_QK_DIM, RET_V_DIM), F32),
        ],
        compiler_params=_cparams(("parallel", "arbitrary")),
        name="retention",
    )(logg, proj, proj, proj, proj, gn_g)


def _split_bf16(v):
    hi = v.astype(BF16)
    lo = (v - hi.astype(F32)).astype(BF16)
    return hi, lo


def _ssd_kernel(z_ref, x_ref, b_ref, c_ref, dt_ref, cwx_ref, cwb_ref, cwc_ref,
                cbx_ref, cbb_ref, cbc_ref, bias_ref, alog_ref, dskip_ref, ng_ref,
                o_ref, xs_sc, bm_sc, cm_sc, dtsp_sc, stb_sc, st_sc, stg_sc, *, nc):
    seq = nc * CHUNK
    hpg = SSM_HPG

    def conv_piece(src_ref, w_ref, bias_row, dst_ref, c, width, col0):
        r0 = pl.multiple_of(c * CHUNK, CHUNK)
        prev0 = pl.multiple_of(jnp.maximum(r0 - 16, 0), 16)
        next0 = pl.multiple_of(jnp.minimum(r0 + CHUNK, seq - 16), 16)
        prev = src_ref[pl.ds(prev0, 16), :].astype(F32)[8:16, :]
        nxt = src_ref[pl.ds(next0, 16), :].astype(F32)[0:8, :]
        cols = slice(col0, col0 + width)
        stg_sc[0:CONV_HALO, cols] = jnp.where(c > 0, prev, 0.0)
        stg_sc[CONV_HALO:CONV_HALO + CHUNK, cols] = src_ref[pl.ds(r0, CHUNK), :].astype(F32)
        stg_sc[CONV_HALO + CHUNK:2 * CONV_HALO + CHUNK, cols] = jnp.where(c < nc - 1, nxt, 0.0)
        acc = jnp.zeros((CHUNK, width), F32)
        for t in range(SSM_CONV):
            off = CONV_HALO - SSM_CONV // 2 + t
            acc = acc + stg_sc[off:off + CHUNK, cols] * w_ref[t:t + 1, :]
        dst_ref[pl.ds(r0, CHUNK), :] = jax.nn.silu(acc + bias_row).astype(BF16)

    def conv_body(c, carry):
        conv_piece(x_ref, cwx_ref, cbx_ref[...], xs_sc, c, SSM_GROUP_W, 0)
        conv_piece(b_ref, cwb_ref, cbb_ref[...], bm_sc, c, SSM_STATE, SSM_GROUP_W)
        conv_piece(c_ref, cwc_ref, cbc_ref[...], cm_sc, c, SSM_STATE, SSM_GROUP_W + SSM_STATE)
        return carry

    lax.fori_loop(0, nc, conv_body, 0)

    dtsp_sc[...] = jax.nn.softplus(dt_ref[...] + bias_ref[0])
    a_row = -jnp.exp(alog_ref[0])

    lane = lax.broadcasted_iota(jnp.int32, (CHUNK, LANES), 1)
    is_fwd = lane < hpg
    ii = lax.broadcasted_iota(jnp.int32, (CHUNK, CHUNK), 0)
    jj = lax.broadcasted_iota(jnp.int32, (CHUNK, CHUNK), 1)
    tril = jnp.where(jj <= ii, 1.0, 0.0).astype(F32)
    causal = jj <= ii
    anti = jj >= ii
    er = lax.broadcasted_iota(jnp.int32, (LANES, SSM_GROUP_W), 0)
    ec = lax.broadcasted_iota(jnp.int32, (LANES, SSM_GROUP_W), 1) // SSM_HEAD_DIM
    exp_f = jnp.where(er == ec, 1.0, 0.0).astype(BF16)
    exp_b = jnp.where(er == ec + hpg, 1.0, 0.0).astype(BF16)
    lane_w = lax.broadcasted_iota(jnp.int32, (CHUNK, LANES), 1)
    pair_lo = lane_w < SSM_HEAD_DIM
    tdims = (((0,), (0,)), ((), ()))

    def expand(v, emat):
        hi, lo = _split_bf16(v)
        return (jnp.dot(hi, emat, preferred_element_type=F32)
                + jnp.dot(lo, emat, preferred_element_type=F32))

    def decays(c):
        rows = pl.ds(pl.multiple_of(c * CHUNK, CHUNK), CHUNK)
        dtc = dtsp_sc[rows, :]
        lac = dtc * a_row
        cum = jnp.dot(tril, lac, preferred_element_type=F32, precision=lax.Precision.HIGHEST)
        tot = cum[CHUNK - 1:CHUNK, :]
        excl = cum - lac
        p = jnp.where(is_fwd, cum, -excl)
        din = jnp.exp(jnp.where(is_fwd, cum, tot - excl))
        wgt = jnp.exp(jnp.where(is_fwd, tot - cum, excl)) * dtc
        dchunk = jnp.exp(jnp.broadcast_to(tot, (8, LANES)))
        return rows, dtc, p, din, wgt, dchunk

    st_sc[...] = jnp.zeros_like(st_sc)

    def sweep_back(t, carry):
        c = nc - 1 - t
        rows, _, _, _, wgt, dchunk = decays(c)
        stb_sc[c] = st_sc[...].astype(BF16)
        xw = (xs_sc[rows, :].astype(F32) * expand(wgt, exp_b)).astype(BF16)
        upd = lax.dot_general(bm_sc[rows, :], xw, tdims, preferred_element_type=F32)
        st_sc[...] = st_sc[...] * expand(dchunk, exp_b)[0:1, :] + upd
        return carry

    lax.fori_loop(0, nc, sweep_back, 0)

    st_sc[...] = jnp.zeros_like(st_sc)

    def sweep_fwd(c, carry):
        rows, dtc, p, din, wgt, dchunk = decays(c)
        xs = xs_sc[rows, :]
        bm = bm_sc[rows, :]
        cm = cm_sc[rows, :]
        cm32 = cm.astype(F32)
        cb = lax.dot_general(cm, bm, (((1,), (1,)), ((), ())), preferred_element_type=F32)
        pt = p.T
        dtt = dtc.T
        stf = st_sc[...].astype(BF16)
        stb = stb_sc[c]
        zero16 = jnp.zeros((), BF16)
        pieces = []
        for pr in range(hpg // 2):
            lanes = slice(pr * LANES, (pr + 1) * LANES)
            xs_p = xs[:, lanes]
            stf_p = stf[:, lanes]
            stb_p = stb[:, lanes]
            y_p = jnp.zeros((CHUNK, LANES), F32)
            for sub in range(2):
                hh = 2 * pr + sub
                keep = pair_lo if sub == 0 else jnp.logical_not(pair_lo)
                col_f = jnp.broadcast_to(p[:, hh:hh + 1], (CHUNK, CHUNK))
                col_b = jnp.broadcast_to(p[:, hpg + hh:hpg + hh + 1], (CHUNK, CHUNK))
                w_f = jnp.exp(jnp.where(causal, col_f - pt[hh:hh + 1, :], -jnp.inf))
                w_b = jnp.exp(jnp.where(anti, col_b - pt[hpg + hh:hpg + hh + 1, :], -jnp.inf))
                m_h = cb * (w_f * dtt[hh:hh + 1, :] + w_b * dtt[hpg + hh:hpg + hh + 1, :])
                cs_f = cm32 * jnp.broadcast_to(din[:, hh:hh + 1], (CHUNK, SSM_STATE))
                cs_b = cm32 * jnp.broadcast_to(din[:, hpg + hh:hpg + hh + 1], (CHUNK, SSM_STATE))
                y_p = y_p + jnp.dot(m_h.astype(BF16), jnp.where(keep, xs_p, zero16),
                                    preferred_element_type=F32)
                y_p = y_p + jnp.dot(cs_f.astype(BF16), jnp.where(keep, stf_p, zero16),
                                    preferred_element_type=F32)
                y_p = y_p + jnp.dot(cs_b.astype(BF16), jnp.where(keep, stb_p, zero16),
                                    preferred_element_type=F32)
            pieces.append(y_p)
        y = jnp.concatenate(pieces, axis=1)
        xs32 = xs.astype(F32)
        y = y + dskip_ref[...] * xs32
        y = y * jax.nn.silu(z_ref[rows, :].astype(F32))
        y = y * lax.rsqrt(jnp.mean(y * y, axis=-1, keepdims=True) + EPS)
        o_ref[rows, :] = (y * ng_ref[...]).astype(BF16)
        xw = (xs32 * expand(wgt, exp_f)).astype(BF16)
        upd = lax.dot_general(bm, xw, tdims, preferred_element_type=F32)
        st_sc[...] = st_sc[...] * expand(dchunk, exp_f)[0:1, :] + upd
        return carry

    lax.fori_loop(0, nc, sweep_fwd, 0)


def _ssd(proj, dt, cw_x, cw_b, cw_c, cb_x, cb_b, cb_c, bias_grp, alog_grp, dskip, norm_g, bsz, seq):
    nc = seq // CHUNK
    zb = COL_Z // SSM_GROUP_W
    xb = COL_X // SSM_GROUP_W
    bb = COL_B // SSM_STATE
    cb = COL_C // SSM_STATE
    gw = SSM_GROUP_W
    return pl.pallas_call(
        functools.partial(_ssd_kernel, nc=nc),
        grid=(bsz, SSM_GROUPS),
        in_specs=[
            pl.BlockSpec((seq, gw), lambda b, g: (b, zb + g)),
            pl.BlockSpec((seq, gw), lambda b, g: (b, xb + g)),
            pl.BlockSpec((seq, SSM_STATE), lambda b, g: (b, bb + g)),
            pl.BlockSpec((seq, SSM_STATE), lambda b, g: (b, cb + g)),
            pl.BlockSpec((seq, LANES), lambda b, g: (b, g)),
            pl.BlockSpec((SSM_CONV, gw), lambda b, g: (0, g)),
            pl.BlockSpec((SSM_CONV, SSM_STATE), lambda b, g: (0, g)),
            pl.BlockSpec((SSM_CONV, SSM_STATE), lambda b, g: (0, g)),
            pl.BlockSpec((1, gw), lambda b, g: (0, g)),
            pl.BlockSpec((1, SSM_STATE), lambda b, g: (0, g)),
            pl.BlockSpec((1, SSM_STATE), lambda b, g: (0, g)),
            pl.BlockSpec((1, 1, LANES), lambda b, g: (g, 0, 0)),
            pl.BlockSpec((1, 1, LANES), lambda b, g: (g, 0, 0)),
            pl.BlockSpec((1, gw), lambda b, g: (0, g)),
            pl.BlockSpec((1, gw), lambda b, g: (0, g)),
        ],
        out_specs=pl.BlockSpec((seq, gw), lambda b, g: (b, g)),
        out_shape=jax.ShapeDtypeStruct((bsz * seq, SSM_INNER), BF16),
        scratch_shapes=[
            pltpu.VMEM((seq, gw), BF16),
            pltpu.VMEM((seq, SSM_STATE), BF16),
            pltpu.VMEM((seq, SSM_STATE), BF16),
            pltpu.VMEM((seq, LANES), F32),
            pltpu.VMEM((nc, SSM_STATE, gw), BF16),
            pltpu.VMEM((SSM_STATE, gw), F32),
            pltpu.VMEM((CHUNK + 2 * CONV_HALO, gw + 2 * SSM_STATE), F32),
        ],
        compiler_params=_cparams(("parallel", "arbitrary")),
        name="ssd",
    )(proj, proj, proj, proj, dt, cw_x, cw_b, cw_c, cb_x, cb_b, cb_c, bias_grp, alog_grp,
      dskip, norm_g)


def _merge_kernel(yr_ref, ys_ref, gr_ref, gs_ref, x_ref, wr_ref, ws_ref, wo_ref, o_ref):
    y_ret = jnp.dot(yr_ref[...], wr_ref[...], preferred_element_type=F32)
    y_ssm = jnp.dot(ys_ref[...], ws_ref[...], preferred_element_type=F32)
    mixed = (jax.nn.sigmoid(gr_ref[...].astype(F32)) * y_ret
             + jax.nn.sigmoid(gs_ref[...].astype(F32)) * y_ssm)
    o_ref[...] = x_ref[...] + jnp.dot(mixed.astype(BF16), wo_ref[...], preferred_element_type=F32)


def _resident(shape):
    return pl.BlockSpec(shape, lambda i: (0,) * len(shape), pipeline_mode=pl.Buffered(1))


def _merge(yret, yssm, proj, x2, w_ret_o, w_ssm_o, w_out, tm):
    m = x2.shape[0]
    grb = COL_GATE_RET // D_MODEL
    gsb = COL_GATE_SSM // D_MODEL
    return pl.pallas_call(
        _merge_kernel,
        grid=(m // tm,),
        in_specs=[
            pl.BlockSpec((tm, RET_V), lambda i: (i, 0)),
            pl.BlockSpec((tm, SSM_INNER), lambda i: (i, 0)),
            pl.BlockSpec((tm, D_MODEL), lambda i: (i, grb)),
            pl.BlockSpec((tm, D_MODEL), lambda i: (i, gsb)),
            pl.BlockSpec((tm, D_MODEL), lambda i: (i, 0)),
            _resident((RET_V, D_MODEL)),
            _resident((SSM_INNER, D_MODEL)),
            _resident((D_MODEL, D_MODEL)),
        ],
        out_specs=pl.BlockSpec((tm, D_MODEL), lambda i: (i, 0)),
        out_shape=jax.ShapeDtypeStruct((m, D_MODEL), F32),
        compiler_params=_cparams(("parallel",)),
        name="merge",
    )(yret, yssm, proj, proj, x2, w_ret_o, w_ssm_o, w_out)


def _rms(x, g):
    return (x * lax.rsqrt(jnp.mean(x * x, axis=-1, keepdims=True) + EPS)) * g


def _mlp_kernel(x_ref, gm_ref, wu_ref, wd_ref, gf_ref, o_ref):
    x = x_ref[...]
    h = _rms(x, gm_ref[...]).astype(BF16)
    up = jnp.dot(h, wu_ref[...], preferred_element_type=F32)
    act = jnp.square(jnp.maximum(up, 0.0)).astype(BF16)
    x2 = x + jnp.dot(act, wd_ref[...], preferred_element_type=F32)
    o_ref[...] = _rms(x2, gf_ref[...])


def _mlp(x1, g_mlp, w_up, w_down, g_final, tm):
    m = x1.shape[0]
    return pl.pallas_call(
        _mlp_kernel,
        grid=(m // tm,),
        in_specs=[
            pl.BlockSpec((tm, D_MODEL), lambda i: (i, 0)),
            _resident((1, D_MODEL)),
            _resident((D_MODEL, D_FF)),
            _resident((D_FF, D_MODEL)),
            _resident((1, D_MODEL)),
        ],
        out_specs=pl.BlockSpec((tm, D_MODEL), lambda i: (i, 0)),
        out_shape=jax.ShapeDtypeStruct((m, D_MODEL), F32),
        compiler_params=_cparams(("parallel",)),
        name="mlp",
    )(x1, g_mlp, w_up, w_down, g_final)


def _group_rows(fwd, bwd):
    f = fwd.reshape(SSM_GROUPS, SSM_HPG)
    b = bwd.reshape(SSM_GROUPS, SSM_HPG)
    pad = jnp.zeros((SSM_GROUPS, LANES - 2 * SSM_HPG), F32)
    return jnp.concatenate([f, b, pad], axis=1).reshape(SSM_GROUPS, 1, LANES)


def _layer(x2, pos2, bsz, seq, norm_mix_g, w_in, ret_gn_g, w_ret_o, conv_w, conv_b, dt_bias_f,
           dt_bias_b, a_log_f, a_log_b, ssm_d, ssm_norm_g, w_ssm_o, w_out, norm_mlp_g, w_mlp_up,
           w_mlp_down, g_final):
    m = bsz * seq
    dt0 = COL_C + SSM_GROUPS * SSM_STATE
    w_main = jnp.concatenate([w_in[:, :dt0], w_in[:, dt0 + 2 * SSM_HEADS:]], axis=1).astype(BF16)
    wdt = w_in[:, dt0:dt0 + 2 * SSM_HEADS]
    wdt_f = wdt[:, :SSM_HEADS].reshape(D_MODEL, SSM_GROUPS, SSM_HPG)
    wdt_b = wdt[:, SSM_HEADS:].reshape(D_MODEL, SSM_GROUPS, SSM_HPG)
    wdt_pad = jnp.zeros((D_MODEL, SSM_GROUPS, LANES - 2 * SSM_HPG), F32)
    w_dt = jnp.concatenate([wdt_f, wdt_b, wdt_pad], axis=2).reshape(D_MODEL, SSM_GROUPS * LANES)
    w_dt = w_dt.astype(BF16)

    inv = (ROPE_BASE ** (-jnp.arange(ROPE_HALF, dtype=F32) / ROPE_HALF)).reshape(1, ROPE_HALF)
    tm1 = min(1024, m)
    proj, dt = _inproj(x2, pos2, inv, norm_mix_g.reshape(1, D_MODEL), w_main, w_dt, tm1)

    logg = jnp.log1p(-jnp.exp2(-5.0 - jnp.arange(RET_HEADS, dtype=F32)))
    yret = _retention(proj, logg, ret_gn_g.reshape(1, RET_V), bsz, seq)

    nbc = SSM_GROUPS * SSM_STATE
    cw_x, cw_b, cw_c = (conv_w[:, :SSM_INNER], conv_w[:, SSM_INNER:SSM_INNER + nbc],
                        conv_w[:, SSM_INNER + nbc:])
    cb2 = conv_b.reshape(1, -1)
    cb_x, cb_b, cb_c = (cb2[:, :SSM_INNER], cb2[:, SSM_INNER:SSM_INNER + nbc],
                        cb2[:, SSM_INNER + nbc:])
    dskip = jnp.repeat(ssm_d, SSM_HEAD_DIM).reshape(1, SSM_INNER)
    yssm = _ssd(proj, dt, cw_x, cw_b, cw_c, cb_x, cb_b, cb_c,
                _group_rows(dt_bias_f, dt_bias_b), _group_rows(a_log_f, a_log_b),
                dskip, ssm_norm_g.reshape(1, SSM_INNER), bsz, seq)

    tm4 = min(512, m)
    x1 = _merge(yret, yssm, proj, x2, w_ret_o.astype(BF16), w_ssm_o.astype(BF16),
                w_out.astype(BF16), tm4)
    return _mlp(x1, norm_mlp_g.reshape(1, D_MODEL), w_mlp_up.astype(BF16),
                w_mlp_down.astype(BF16), g_final, tm4)


def kernel(x, positions, norm_mix_g, w_in, ret_gn_g, w_ret_o, conv_w, conv_b, dt_bias_f, dt_bias_b,
           a_log_f, a_log_b, ssm_d, ssm_norm_g, w_ssm_o, w_out, norm_mlp_g, w_mlp_up, w_mlp_down,
           norm_final_g):
    bsz, seq, _ = x.shape
    depth = w_in.shape[0]
    assert depth == 1, "the final-norm fusion in the MLP call assumes a single layer"
    x2 = x.reshape(bsz * seq, D_MODEL)
    pos2 = positions.reshape(bsz * seq, 1)
    out = _layer(x2, pos2, bsz, seq, norm_mix_g[0], w_in[0], ret_gn_g[0], w_ret_o[0], conv_w[0],
                 conv_b[0], dt_bias_f[0], dt_bias_b[0], a_log_f[0], a_log_b[0], ssm_d[0],
                 ssm_norm_g[0], w_ssm_o[0], w_out[0], norm_mlp_g[0], w_mlp_up[0], w_mlp_down[0],
                 norm_final_g.reshape(1, D_MODEL))
    return out.reshape(bsz, seq, D_MODEL)
```

```python
import functools

import jax
import jax.numpy as jnp
from jax import lax
from jax.experimental import pallas as pl
from jax.experimental.pallas import tpu as pltpu

F32 = jnp.float32
BF16 = jnp.bfloat16

D_MODEL = 1024
RET_HEADS = 4
RET_QK_DIM = 256
RET_V_DIM = 512
RET_QK = RET_HEADS * RET_QK_DIM
RET_V = RET_HEADS * RET_V_DIM
ROPE_BASE = 10000.0
ROPE_HALF = RET_QK_DIM // 2
SSM_INNER = 2 * D_MODEL
SSM_HEAD_DIM = 64
SSM_HEADS = SSM_INNER // SSM_HEAD_DIM
SSM_GROUPS = 4
SSM_HPG = SSM_HEADS // SSM_GROUPS
SSM_STATE = 128
SSM_CONV = 5
SSM_GROUP_W = SSM_HPG * SSM_HEAD_DIM
D_FF = 4 * D_MODEL
EPS = 1e-6
CHUNK = 128

LANES = 128
CONV_HALO = 8

COL_Q = 0
COL_K = COL_Q + RET_QK
COL_V = COL_K + RET_QK
COL_G = COL_V + RET_V
COL_Z = COL_G + RET_V
COL_X = COL_Z + SSM_INNER
COL_B = COL_X + SSM_INNER
COL_C = COL_B + SSM_GROUPS * SSM_STATE
COL_GATE_RET = COL_C + SSM_GROUPS * SSM_STATE
COL_GATE_SSM = COL_GATE_RET + D_MODEL
PROJ_W = COL_GATE_SSM + D_MODEL

VMEM_LIMIT = 56 * 1024 * 1024


def _cparams(sem):
    return pltpu.CompilerParams(dimension_semantics=sem, vmem_limit_bytes=VMEM_LIMIT)


def _inproj_kernel(x_ref, pos_ref, inv_ref, g_ref, w_ref, wdt_ref, proj_ref, dt_ref,
                   h_sc, cos_sc, sin_sc):
    j = pl.program_id(1)

    @pl.when(j == 0)
    def _():
        x = x_ref[...]
        ms = jnp.mean(x * x, axis=-1, keepdims=True)
        h = ((x * lax.rsqrt(ms + EPS)) * g_ref[...]).astype(BF16)
        h_sc[...] = h
        dt_ref[...] = jnp.dot(h, wdt_ref[...], preferred_element_type=F32)
        ang = pos_ref[...].astype(F32) * inv_ref[...]
        cos_sc[...] = jnp.cos(ang)
        sin_sc[...] = jnp.sin(ang)

    @pl.when(j < 2)
    def _():
        acc = jnp.dot(h_sc[...], w_ref[...], preferred_element_type=F32)
        scale = jnp.where(j == 0, 1.0, RET_QK_DIM ** -0.5).astype(F32)
        c = cos_sc[...]
        s = sin_sc[...]
        for hd in range(RET_HEADS):
            lo = hd * RET_QK_DIM
            t1 = acc[:, lo:lo + ROPE_HALF]
            t2 = acc[:, lo + ROPE_HALF:lo + RET_QK_DIM]
            proj_ref[:, lo:lo + ROPE_HALF] = ((t1 * c - t2 * s) * scale).astype(BF16)
            proj_ref[:, lo + ROPE_HALF:lo + RET_QK_DIM] = ((t1 * s + t2 * c) * scale).astype(BF16)

    @pl.when(j >= 2)
    def _():
        proj_ref[...] = jnp.dot(h_sc[...], w_ref[...], preferred_element_type=F32).astype(BF16)


def _inproj(x2, pos2, inv, g, w_main, w_dt, tm):
    m = x2.shape[0]
    tn = RET_QK
    grid = (m // tm, PROJ_W // tn)
    return pl.pallas_call(
        _inproj_kernel,
        grid=grid,
        in_specs=[
            pl.BlockSpec((tm, D_MODEL), lambda i, j: (i, 0)),
            pl.BlockSpec((tm, 1), lambda i, j: (i, 0)),
            pl.BlockSpec((1, ROPE_HALF), lambda i, j: (0, 0)),
            pl.BlockSpec((1, D_MODEL), lambda i, j: (0, 0)),
            pl.BlockSpec((D_MODEL, tn), lambda i, j: (0, j)),
            pl.BlockSpec((D_MODEL, SSM_GROUPS * LANES), lambda i, j: (0, 0)),
        ],
        out_specs=[
            pl.BlockSpec((tm, tn), lambda i, j: (i, j)),
            pl.BlockSpec((tm, SSM_GROUPS * LANES), lambda i, j: (i, 0)),
        ],
        out_shape=[
            jax.ShapeDtypeStruct((m, PROJ_W), BF16),
            jax.ShapeDtypeStruct((m, SSM_GROUPS * LANES), F32),
        ],
        scratch_shapes=[
            pltpu.VMEM((tm, D_MODEL), BF16),
            pltpu.VMEM((tm, ROPE_HALF), F32),
            pltpu.VMEM((tm, ROPE_HALF), F32),
        ],
        compiler_params=_cparams(("parallel", "arbitrary")),
        name="inproj",
    )(x2, pos2, inv, g, w_main, w_dt)


def _ret_kernel(logg_ref, q_ref, k_ref, v_ref, g_ref, gn_ref, o_ref, sb_sc, st_sc, *, nc):
    lg = logg_ref[pl.program_id(1)]
    pos = lax.broadcasted_iota(jnp.int32, (CHUNK, RET_QK_DIM), 0).astype(F32)
    qdec_f = jnp.exp((pos + 1.0) * lg)
    kdec_f = jnp.exp((CHUNK - 1.0 - pos) * lg)
    qdec_b = jnp.exp((CHUNK - pos) * lg)
    kdec_b = jnp.exp(pos * lg)
    chunk_dec = jnp.exp(jnp.full((1, RET_V_DIM), float(CHUNK), F32) * lg)
    ii = lax.broadcasted_iota(jnp.int32, (CHUNK, CHUNK), 0)
    jj = lax.broadcasted_iota(jnp.int32, (CHUNK, CHUNK), 1)
    intra_dec = jnp.exp(jnp.abs(ii - jj).astype(F32) * lg)
    tdims = (((0,), (0,)), ((), ()))

    st_sc[...] = jnp.zeros_like(st_sc)

    def sweep_back(t, carry):
        c = nc - 1 - t
        rows = pl.ds(pl.multiple_of(c * CHUNK, CHUNK), CHUNK)
        sb_sc[c] = st_sc[...].astype(BF16)
        kb = (k_ref[rows, :].astype(F32) * kdec_b).astype(BF16)
        upd = lax.dot_general(kb, v_ref[rows, :], tdims, preferred_element_type=F32)
        st_sc[...] = st_sc[...] * chunk_dec + upd
        return carry

    lax.fori_loop(0, nc, sweep_back, 0, unroll=2)

    st_sc[...] = jnp.zeros_like(st_sc)

    def sweep_fwd(c, carry):
        rows = pl.ds(pl.multiple_of(c * CHUNK, CHUNK), CHUNK)
        qb16 = q_ref[rows, :]
        kb16 = k_ref[rows, :]
        vc = v_ref[rows, :]
        qf32 = qb16.astype(F32)
        scores = lax.dot_general(qb16, kb16, (((1,), (1,)), ((), ())), preferred_element_type=F32)
        y = jnp.dot((scores * intra_dec).astype(BF16), vc, preferred_element_type=F32)
        y = y + jnp.dot((qf32 * qdec_f).astype(BF16), st_sc[...].astype(BF16),
                        preferred_element_type=F32)
        y = y + jnp.dot((qf32 * qdec_b).astype(BF16), sb_sc[c], preferred_element_type=F32)
        mu = jnp.mean(y, axis=-1, keepdims=True)
        d = y - mu
        var = jnp.mean(d * d, axis=-1, keepdims=True)
        yn = (d * lax.rsqrt(var + EPS)) * gn_ref[...]
        gate = g_ref[rows, :].astype(F32)
        o_ref[rows, :] = (yn * jax.nn.silu(gate)).astype(BF16)
        kf = (kb16.astype(F32) * kdec_f).astype(BF16)
        upd = lax.dot_general(kf, vc, tdims, preferred_element_type=F32)
        st_sc[...] = st_sc[...] * chunk_dec + upd
        return carry

    lax.fori_loop(0, nc, sweep_fwd, 0, unroll=2)


def _retention(proj, logg, gn_g, bsz, seq):
    nc = seq // CHUNK
    qb = COL_Q // RET_QK_DIM
    kb = COL_K // RET_QK_DIM
    vb = COL_V // RET_V_DIM
    gb = COL_G // RET_V_DIM
    return pl.pallas_call(
        functools.partial(_ret_kernel, nc=nc),
        grid=(bsz, RET_HEADS),
        in_specs=[
            pl.BlockSpec(memory_space=pltpu.SMEM),
            pl.BlockSpec((seq, RET_QK_DIM), lambda b, h: (b, qb + h)),
            pl.BlockSpec((seq, RET_QK_DIM), lambda b, h: (b, kb + h)),
            pl.BlockSpec((seq, RET_V_DIM), lambda b, h: (b, vb + h)),
            pl.BlockSpec((seq, RET_V_DIM), lambda b, h: (b, gb + h)),
            pl.BlockSpec((1, RET_V_DIM), lambda b, h: (0, h)),
        ],
        out_specs=pl.BlockSpec((seq, RET_V_DIM), lambda b, h: (b, h)),
        out_shape=jax.ShapeDtypeStruct((bsz * seq, RET_V), BF16),
        scratch_shapes=[
            pltpu.VMEM((nc, RET_QK_DIM, RET_V_DIM), BF16),
            pltpu.VMEM((RET_QK_DIM, RET_V_DIM), F32),
        ],
        compiler_params=_cparams(("parallel", "arbitrary")),
        name="retention",
    )(logg, proj, proj, proj, proj, gn_g)


def _split_bf16(v):
    hi = v.astype(BF16)
    lo = (v - hi.astype(F32)).astype(BF16)
    return hi, lo


def _ssd_kernel(z_ref, x_ref, b_ref, c_ref, dt_ref, cwx_ref, cwb_ref, cwc_ref,
                cbx_ref, cbb_ref, cbc_ref, bias_ref, alog_ref, dskip_ref, ng_ref,
                o_ref, xs_sc, bm_sc, cm_sc, dtsp_sc, stb_sc, st_sc, stg_sc, e_sc, *, nc):
    seq = nc * CHUNK
    hpg = SSM_HPG
    gw = SSM_GROUP_W

    def conv_piece(src_ref, w_ref, bias_row, dst_ref, c, width, col0):
        r0 = pl.multiple_of(c * CHUNK, CHUNK)
        prev0 = pl.multiple_of(jnp.maximum(r0 - 16, 0), 16)
        next0 = pl.multiple_of(jnp.minimum(r0 + CHUNK, seq - 16), 16)
        prev = src_ref[pl.ds(prev0, 16), :].astype(F32)[8:16, :]
        nxt = src_ref[pl.ds(next0, 16), :].astype(F32)[0:8, :]
        cols = slice(col0, col0 + width)
        stg_sc[0:CONV_HALO, cols] = jnp.where(c > 0, prev, 0.0)
        stg_sc[CONV_HALO:CONV_HALO + CHUNK, cols] = src_ref[pl.ds(r0, CHUNK), :].astype(F32)
        stg_sc[CONV_HALO + CHUNK:2 * CONV_HALO + CHUNK, cols] = jnp.where(c < nc - 1, nxt, 0.0)
        acc = jnp.zeros((CHUNK, width), F32)
        for t in range(SSM_CONV):
            off = CONV_HALO - SSM_CONV // 2 + t
            acc = acc + stg_sc[off:off + CHUNK, cols] * w_ref[t:t + 1, :]
        dst_ref[pl.ds(r0, CHUNK), :] = jax.nn.silu(acc + bias_row).astype(BF16)

    def conv_body(c, carry):
        conv_piece(x_ref, cwx_ref, cbx_ref[...], xs_sc, c, SSM_GROUP_W, 0)
        conv_piece(b_ref, cwb_ref, cbb_ref[...], bm_sc, c, SSM_STATE, SSM_GROUP_W)
        conv_piece(c_ref, cwc_ref, cbc_ref[...], cm_sc, c, SSM_STATE, SSM_GROUP_W + SSM_STATE)
        return carry

    lax.fori_loop(0, nc, conv_body, 0)

    dtsp_sc[...] = jax.nn.softplus(dt_ref[...] + bias_ref[0])
    a_row = -jnp.exp(alog_ref[0])

    lane = lax.broadcasted_iota(jnp.int32, (CHUNK, LANES), 1)
    is_fwd = lane < hpg
    ii = lax.broadcasted_iota(jnp.int32, (CHUNK, CHUNK), 0)
    jj = lax.broadcasted_iota(jnp.int32, (CHUNK, CHUNK), 1)
    tril = jnp.where(jj <= ii, 1.0, 0.0).astype(F32)
    causal = jj <= ii
    diag = ii == jj
    pair_lo = lane < SSM_HEAD_DIM
    tdims = (((0,), (0,)), ((), ()))

    er = lax.broadcasted_iota(jnp.int32, (2 * LANES, 2 * gw), 0) % LANES
    ec = lax.broadcasted_iota(jnp.int32, (2 * LANES, 2 * gw), 1) // SSM_HEAD_DIM
    e_sc[...] = jnp.where(er == ec, 1.0, 0.0).astype(BF16)

    def expand(v, col0, ncol):
        hi, lo = _split_bf16(v)
        return jnp.dot(jnp.concatenate([hi, lo], axis=1), e_sc[:, col0:col0 + ncol],
                       preferred_element_type=F32)

    def decays(c):
        rows = pl.ds(pl.multiple_of(c * CHUNK, CHUNK), CHUNK)
        dtc = dtsp_sc[rows, :]
        lac = dtc * a_row
        cum = jnp.dot(tril, lac, preferred_element_type=F32, precision=lax.Precision.HIGHEST)
        tot = cum[CHUNK - 1:CHUNK, :]
        excl = cum - lac
        wgt = jnp.exp(jnp.where(is_fwd, tot - cum, excl)) * dtc
        dchunk = jnp.exp(jnp.broadcast_to(tot, (8, LANES)))
        return rows, dtc, cum, excl, tot, wgt, dchunk

    st_sc[...] = jnp.zeros_like(st_sc)

    def sweep_back(t, carry):
        c = nc - 1 - t
        rows, _, _, _, _, wgt, dchunk = decays(c)
        stb_sc[c] = st_sc[...].astype(BF16)
        xw = (xs_sc[rows, :].astype(F32) * expand(wgt, gw, gw)).astype(BF16)
        upd = lax.dot_general(bm_sc[rows, :], xw, tdims, preferred_element_type=F32)
        st_sc[...] = st_sc[...] * expand(dchunk, gw, gw)[0:1, :] + upd
        return carry

    lax.fori_loop(0, nc, sweep_back, 0, unroll=2)

    st_sc[...] = jnp.zeros_like(st_sc)

    def sweep_fwd(c, carry):
        rows, dtc, cum, excl, tot, wgt, dchunk = decays(c)
        pcol = jnp.where(is_fwd, cum, -excl)
        prow_t = (pcol - jnp.log(dtc)).T
        din = jnp.exp(jnp.where(is_fwd, cum, tot - excl))
        xs = xs_sc[rows, :]
        bm = bm_sc[rows, :]
        cm = cm_sc[rows, :]
        cb = lax.dot_general(cm, bm, (((1,), (1,)), ((), ())), preferred_element_type=F32)
        cb_diag = jnp.sum(jnp.where(diag, cb, 0.0), axis=1, keepdims=True)
        ue = expand(jnp.where(is_fwd, wgt, cb_diag * dtc), 0, 2 * gw)
        de = expand(din, 0, 2 * gw)
        y_st = (jnp.dot(cm, st_sc[...].astype(BF16), preferred_element_type=F32) * de[:, :gw]
                + jnp.dot(cm, stb_sc[c], preferred_element_type=F32) * de[:, gw:])
        zero16 = jnp.zeros((), BF16)
        pieces = []
        for pr in range(hpg // 2):
            xs_p = xs[:, pr * LANES:(pr + 1) * LANES]
            y_p = jnp.zeros((CHUNK, LANES), F32)
            for sub in range(2):
                hh = 2 * pr + sub
                keep = pair_lo if sub == 0 else jnp.logical_not(pair_lo)
                col_f = jnp.broadcast_to(pcol[:, hh:hh + 1], (CHUNK, CHUNK))
                col_b = jnp.broadcast_to(pcol[:, hpg + hh:hpg + hh + 1], (CHUNK, CHUNK))
                arg = jnp.where(causal, col_f - prow_t[hh:hh + 1, :],
                                col_b - prow_t[hpg + hh:hpg + hh + 1, :])
                m_h = (cb * jnp.exp(arg)).astype(BF16)
                y_p = y_p + jnp.dot(m_h, jnp.where(keep, xs_p, zero16),
                                    preferred_element_type=F32)
            pieces.append(y_p)
        xs32 = xs.astype(F32)
        y = jnp.concatenate(pieces, axis=1) + y_st + (dskip_ref[...] + ue[:, gw:]) * xs32
        y = y * jax.nn.silu(z_ref[rows, :].astype(F32))
        y = y * lax.rsqrt(jnp.mean(y * y, axis=-1, keepdims=True) + EPS)
        o_ref[rows, :] = (y * ng_ref[...]).astype(BF16)
        xw = (xs32 * ue[:, :gw]).astype(BF16)
        upd = lax.dot_general(bm, xw, tdims, preferred_element_type=F32)
        st_sc[...] = st_sc[...] * expand(dchunk, 0, gw)[0:1, :] + upd
        return carry

    lax.fori_loop(0, nc, sweep_fwd, 0, unroll=2)


def _ssd(proj, dt, cw_x, cw_b, cw_c, cb_x, cb_b, cb_c, bias_grp, alog_grp, dskip, norm_g, bsz, seq):
    nc = seq // CHUNK
    zb = COL_Z // SSM_GROUP_W
    xb = COL_X // SSM_GROUP_W
    bb = COL_B // SSM_STATE
    cb = COL_C // SSM_STATE
    gw = SSM_GROUP_W
    return pl.pallas_call(
        functools.partial(_ssd_kernel, nc=nc),
        grid=(bsz, SSM_GROUPS),
        in_specs=[
            pl.BlockSpec((seq, gw), lambda b, g: (b, zb + g)),
            pl.BlockSpec((seq, gw), lambda b, g: (b, xb + g)),
            pl.BlockSpec((seq, SSM_STATE), lambda b, g: (b, bb + g)),
            pl.BlockSpec((seq, SSM_STATE), lambda b, g: (b, cb + g)),
            pl.BlockSpec((seq, LANES), lambda b, g: (b, g)),
            pl.BlockSpec((SSM_CONV, gw), lambda b, g: (0, g)),
            pl.BlockSpec((SSM_CONV, SSM_STATE), lambda b, g: (0, g)),
            pl.BlockSpec((SSM_CONV, SSM_STATE), lambda b, g: (0, g)),
            pl.BlockSpec((1, gw), lambda b, g: (0, g)),
            pl.BlockSpec((1, SSM_STATE), lambda b, g: (0, g)),
            pl.BlockSpec((1, SSM_STATE), lambda b, g: (0, g)),
            pl.BlockSpec((1, 1, LANES), lambda b, g: (g, 0, 0)),
            pl.BlockSpec((1, 1, LANES), lambda b, g: (g, 0, 0)),
            pl.BlockSpec((1, gw), lambda b, g: (0, g)),
            pl.BlockSpec((1, gw), lambda b, g: (0, g)),
        ],
        out_specs=pl.BlockSpec((seq, gw), lambda b, g: (b, g)),
        out_shape=jax.ShapeDtypeStruct((bsz * seq, SSM_INNER), BF16),
        scratch_shapes=[
            pltpu.VMEM((seq, gw), BF16),
            pltpu.VMEM((seq, SSM_STATE), BF16),
            pltpu.VMEM((seq, SSM_STATE), BF16),
            pltpu.VMEM((seq, LANES), F32),
            pltpu.VMEM((nc, SSM_STATE, gw), BF16),
            pltpu.VMEM((SSM_STATE, gw), F32),
            pltpu.VMEM((CHUNK + 2 * CONV_HALO, gw + 2 * SSM_STATE), F32),
            pltpu.VMEM((2 * LANES, 2 * gw), BF16),
        ],
        compiler_params=_cparams(("parallel", "arbitrary")),
        name="ssd",
    )(proj, proj, proj, proj, dt, cw_x, cw_b, cw_c, cb_x, cb_b, cb_c, bias_grp, alog_grp,
      dskip, norm_g)


def _merge_kernel(yr_ref, ys_ref, gr_ref, gs_ref, x_ref, wr_ref, ws_ref, wo_ref, o_ref):
    y_ret = jnp.dot(yr_ref[...], wr_ref[...], preferred_element_type=F32)
    y_ssm = jnp.dot(ys_ref[...], ws_ref[...], preferred_element_type=F32)
    mixed = (jax.nn.sigmoid(gr_ref[...].astype(F32)) * y_ret
             + jax.nn.sigmoid(gs_ref[...].astype(F32)) * y_ssm)
    o_ref[...] = x_ref[...] + jnp.dot(mixed.astype(BF16), wo_ref[...], preferred_element_type=F32)


def _resident(shape):
    return pl.BlockSpec(shape, lambda i: (0,) * len(shape), pipeline_mode=pl.Buffered(1))


def _merge(yret, yssm, proj, x2, w_ret_o, w_ssm_o, w_out, tm):
    m = x2.shape[0]
    grb = COL_GATE_RET // D_MODEL
    gsb = COL_GATE_SSM // D_MODEL
    return pl.pallas_call(
        _merge_kernel,
        grid=(m // tm,),
        in_specs=[
            pl.BlockSpec((tm, RET_V), lambda i: (i, 0)),
            pl.BlockSpec((tm, SSM_INNER), lambda i: (i, 0)),
            pl.BlockSpec((tm, D_MODEL), lambda i: (i, grb)),
            pl.BlockSpec((tm, D_MODEL), lambda i: (i, gsb)),
            pl.BlockSpec((tm, D_MODEL), lambda i: (i, 0)),
            _resident((RET_V, D_MODEL)),
            _resident((SSM_INNER, D_MODEL)),
            _resident((D_MODEL, D_MODEL)),
        ],
        out_specs=pl.BlockSpec((tm, D_MODEL), lambda i: (i, 0)),
        out_shape=jax.ShapeDtypeStruct((m, D_MODEL), F32),
        compiler_params=_cparams(("parallel",)),
        name="merge",
    )(yret, yssm, proj, proj, x2, w_ret_o, w_ssm_o, w_out)


def _rms(x, g):
    return (x * lax.rsqrt(jnp.mean(x * x, axis=-1, keepdims=True) + EPS)) * g


def _mlp_kernel(x_ref, gm_ref, wu_ref, wd_ref, gf_ref, o_ref):
    x = x_ref[...]
    h = _rms(x, gm_ref[...]).astype(BF16)
    up = jnp.dot(h, wu_ref[...], preferred_element_type=F32)
    act = jnp.square(jnp.maximum(up, 0.0)).astype(BF16)
    x2 = x + jnp.dot(act, wd_ref[...], preferred_element_type=F32)
    o_ref[...] = _rms(x2, gf_ref[...])


def _mlp(x1, g_mlp, w_up, w_down, g_final, tm):
    m = x1.shape[0]
    return pl.pallas_call(
        _mlp_kernel,
        grid=(m // tm,),
        in_specs=[
            pl.BlockSpec((tm, D_MODEL), lambda i: (i, 0)),
            _resident((1, D_MODEL)),
            _resident((D_MODEL, D_FF)),
            _resident((D_FF, D_MODEL)),
            _resident((1, D_MODEL)),
        ],
        out_specs=pl.BlockSpec((tm, D_MODEL), lambda i: (i, 0)),
        out_shape=jax.ShapeDtypeStruct((m, D_MODEL), F32),
        compiler_params=_cparams(("parallel",)),
        name="mlp",
    )(x1, g_mlp, w_up, w_down, g_final)


def _group_rows(fwd, bwd):
    f = fwd.reshape(SSM_GROUPS, SSM_HPG)
    b = bwd.reshape(SSM_GROUPS, SSM_HPG)
    pad = jnp.zeros((SSM_GROUPS, LANES - 2 * SSM_HPG), F32)
    return jnp.concatenate([f, b, pad], axis=1).reshape(SSM_GROUPS, 1, LANES)


def _layer(x2, pos2, bsz, seq, norm_mix_g, w_in, ret_gn_g, w_ret_o, conv_w, conv_b, dt_bias_f,
           dt_bias_b, a_log_f, a_log_b, ssm_d, ssm_norm_g, w_ssm_o, w_out, norm_mlp_g, w_mlp_up,
           w_mlp_down, g_final):
    m = bsz * seq
    dt0 = COL_C + SSM_GROUPS * SSM_STATE
    w_main = jnp.concatenate([w_in[:, :dt0], w_in[:, dt0 + 2 * SSM_HEADS:]], axis=1).astype(BF16)
    wdt = w_in[:, dt0:dt0 + 2 * SSM_HEADS]
    wdt_f = wdt[:, :SSM_HEADS].reshape(D_MODEL, SSM_GROUPS, SSM_HPG)
    wdt_b = wdt[:, SSM_HEADS:].reshape(D_MODEL, SSM_GROUPS, SSM_HPG)
    wdt_pad = jnp.zeros((D_MODEL, SSM_GROUPS, LANES - 2 * SSM_HPG), F32)
    w_dt = jnp.concatenate([wdt_f, wdt_b, wdt_pad], axis=2).reshape(D_MODEL, SSM_GROUPS * LANES)
    w_dt = w_dt.astype(BF16)

    inv = (ROPE_BASE ** (-jnp.arange(ROPE_HALF, dtype=F32) / ROPE_HALF)).reshape(1, ROPE_HALF)
    tm1 = min(1024, m)
    proj, dt = _inproj(x2, pos2, inv, norm_mix_g.reshape(1, D_MODEL), w_main, w_dt, tm1)

    logg = jnp.log1p(-jnp.exp2(-5.0 - jnp.arange(RET_HEADS, dtype=F32)))
    yret = _retention(proj, logg, ret_gn_g.reshape(1, RET_V), bsz, seq)

    nbc = SSM_GROUPS * SSM_STATE
    cw_x, cw_b, cw_c = (conv_w[:, :SSM_INNER], conv_w[:, SSM_INNER:SSM_INNER + nbc],
                        conv_w[:, SSM_INNER + nbc:])
    cb2 = conv_b.reshape(1, -1)
    cb_x, cb_b, cb_c = (cb2[:, :SSM_INNER], cb2[:, SSM_INNER:SSM_INNER + nbc],
                        cb2[:, SSM_INNER + nbc:])
    dskip = jnp.repeat(ssm_d, SSM_HEAD_DIM).reshape(1, SSM_INNER)
    yssm = _ssd(proj, dt, cw_x, cw_b, cw_c, cb_x, cb_b, cb_c,
                _group_rows(dt_bias_f, dt_bias_b), _group_rows(a_log_f, a_log_b),
                dskip, ssm_norm_g.reshape(1, SSM_INNER), bsz, seq)

    tm4 = min(512, m)
    x1 = _merge(yret, yssm, proj, x2, w_ret_o.astype(BF16), w_ssm_o.astype(BF16),
                w_out.astype(BF16), tm4)
    return _mlp(x1, norm_mlp_g.reshape(1, D_MODEL), w_mlp_up.astype(BF16),
                w_mlp_down.astype(BF16), g_final, tm4)


def kernel(x, positions, norm_mix_g, w_in, ret_gn_g, w_ret_o, conv_w, conv_b, dt_bias_f, dt_bias_b,
           a_log_f, a_log_b, ssm_d, ssm_norm_g, w_ssm_o, w_out, norm_mlp_g, w_mlp_up, w_mlp_down,
           norm_final_g):
    bsz, seq, _ = x.shape
    depth = w_in.shape[0]
    assert depth == 1, "the final-norm fusion in the MLP call assumes a single layer"
    x2 = x.reshape(bsz * seq, D_MODEL)
    pos2 = positions.reshape(bsz * seq, 1)
    out = _layer(x2, pos2, bsz, seq, norm_mix_g[0], w_in[0], ret_gn_g[0], w_ret_o[0], conv_w[0],
                 conv_b[0], dt_bias_f[0], dt_bias_b[0], a_log_f[0], a_log_b[0], ssm_d[0],
                 ssm_norm_g[0], w_ssm_o[0], w_out[0], norm_mlp_g[0], w_mlp_up[0], w_mlp_down[0],
                 norm_final_g.reshape(1, D_MODEL))
    return out.reshape(bsz, seq, D_MODEL)
```

```python
import functools

import jax
import jax.numpy as jnp
from jax import lax
from jax.experimental import pallas as pl
from jax.experimental.pallas import tpu as pltpu

F32 = jnp.float32
BF16 = jnp.bfloat16

D_MODEL = 1024
RET_HEADS = 4
RET_QK_DIM = 256
RET_V_DIM = 512
RET_QK = RET_HEADS * RET_QK_DIM
RET_V = RET_HEADS * RET_V_DIM
ROPE_BASE = 10000.0
ROPE_HALF = RET_QK_DIM // 2
SSM_INNER = 2 * D_MODEL
SSM_HEAD_DIM = 64
SSM_HEADS = SSM_INNER // SSM_HEAD_DIM
SSM_GROUPS = 4
SSM_HPG = SSM_HEADS // SSM_GROUPS
SSM_STATE = 128
SSM_CONV = 5
SSM_GROUP_W = SSM_HPG * SSM_HEAD_DIM
D_FF = 4 * D_MODEL
EPS = 1e-6
CHUNK = 128
RET_CHUNK = 256

LANES = 128
BF16_ROWS = 16
MXU_K = 256
CONV_PAD = BF16_ROWS
CONV_K = MXU_K
CONV_SIDE_TAPS = tuple(t for t in range(SSM_CONV) if t != SSM_CONV // 2)
PACK_LANES = 2 * SSM_HPG
E_ORIGIN = LANES - PACK_LANES

COL_Q = 0
COL_K = COL_Q + RET_QK
COL_V = COL_K + RET_QK
COL_G = COL_V + RET_V
COL_Z = COL_G + RET_V
COL_X = COL_Z + SSM_INNER
COL_B = COL_X + SSM_INNER
COL_C = COL_B + SSM_GROUPS * SSM_STATE
COL_GATE_RET = COL_C + SSM_GROUPS * SSM_STATE
COL_GATE_SSM = COL_GATE_RET + D_MODEL
PROJ_W = COL_GATE_SSM + D_MODEL

VMEM_LIMIT = 56 * 1024 * 1024


def _cparams(sem):
    return pltpu.CompilerParams(dimension_semantics=sem, vmem_limit_bytes=VMEM_LIMIT)


def _inproj_kernel(x_ref, pos_ref, inv_ref, g_ref, w_ref, wdt_ref, proj_ref, dt_ref,
                   h_sc, cos_sc, sin_sc):
    j = pl.program_id(1)

    @pl.when(j == 0)
    def _():
        x = x_ref[...]
        ms = jnp.mean(x * x, axis=-1, keepdims=True)
        h = ((x * lax.rsqrt(ms + EPS)) * g_ref[...]).astype(BF16)
        h_sc[...] = h
        dt_ref[...] = jnp.dot(h, wdt_ref[...], preferred_element_type=F32)
        ang = pos_ref[...].astype(F32) * inv_ref[...]
        cos_sc[...] = jnp.cos(ang)
        sin_sc[...] = jnp.sin(ang)

    @pl.when(j < 2)
    def _():
        acc = jnp.dot(h_sc[...], w_ref[...], preferred_element_type=F32)
        scale = jnp.where(j == 0, 1.0, RET_QK_DIM ** -0.5).astype(F32)
        c = cos_sc[...]
        s = sin_sc[...]
        for hd in range(RET_HEADS):
            lo = hd * RET_QK_DIM
            t1 = acc[:, lo:lo + ROPE_HALF]
            t2 = acc[:, lo + ROPE_HALF:lo + RET_QK_DIM]
            proj_ref[:, lo:lo + ROPE_HALF] = ((t1 * c - t2 * s) * scale).astype(BF16)
            proj_ref[:, lo + ROPE_HALF:lo + RET_QK_DIM] = ((t1 * s + t2 * c) * scale).astype(BF16)

    @pl.when(j >= 2)
    def _():
        proj_ref[...] = jnp.dot(h_sc[...], w_ref[...], preferred_element_type=F32).astype(BF16)


def _inproj(x2, pos2, inv, g, w_main, w_dt, tm):
    m = x2.shape[0]
    tn = RET_QK
    grid = (m // tm, PROJ_W // tn)
    return pl.pallas_call(
        _inproj_kernel,
        grid=grid,
        in_specs=[
            pl.BlockSpec((tm, D_MODEL), lambda i, j: (i, 0)),
            pl.BlockSpec((tm, 1), lambda i, j: (i, 0)),
            pl.BlockSpec((1, ROPE_HALF), lambda i, j: (0, 0)),
            pl.BlockSpec((1, D_MODEL), lambda i, j: (0, 0)),
            pl.BlockSpec((D_MODEL, tn), lambda i, j: (0, j)),
            pl.BlockSpec((D_MODEL, SSM_GROUPS * LANES), lambda i, j: (0, 0)),
        ],
        out_specs=[
            pl.BlockSpec((tm, tn), lambda i, j: (i, j)),
            pl.BlockSpec((tm, SSM_GROUPS * LANES), lambda i, j: (i, 0)),
        ],
        out_shape=[
            jax.ShapeDtypeStruct((m, PROJ_W), BF16),
            jax.ShapeDtypeStruct((m, SSM_GROUPS * LANES), F32),
        ],
        scratch_shapes=[
            pltpu.VMEM((tm, D_MODEL), BF16),
            pltpu.VMEM((tm, ROPE_HALF), F32),
            pltpu.VMEM((tm, ROPE_HALF), F32),
        ],
        compiler_params=_cparams(("parallel", "arbitrary")),
        name="inproj",
    )(x2, pos2, inv, g, w_main, w_dt)


def _ret_kernel(logg_ref, q_ref, k_ref, v_ref, g_ref, gn_ref, o_ref, sb_sc, st_sc, y_sc, *, nc):
    lg = logg_ref[pl.program_id(1)]
    pos = lax.broadcasted_iota(jnp.int32, (RET_CHUNK, RET_QK_DIM), 0).astype(F32)
    qdec_f = jnp.exp((pos + 1.0) * lg)
    kdec_f = jnp.exp((RET_CHUNK - 1.0 - pos) * lg)
    qdec_b = jnp.exp((RET_CHUNK - pos) * lg)
    kdec_b = jnp.exp(pos * lg)
    chunk_dec = jnp.exp(jnp.full((1, RET_V_DIM), float(RET_CHUNK), F32) * lg)
    ii = lax.broadcasted_iota(jnp.int32, (RET_CHUNK, RET_CHUNK), 0)
    jj = lax.broadcasted_iota(jnp.int32, (RET_CHUNK, RET_CHUNK), 1)
    intra_dec = jnp.exp(jnp.abs(ii - jj).astype(F32) * lg)
    tdims = (((0,), (0,)), ((), ()))

    st_sc[...] = jnp.zeros_like(st_sc)

    def sweep_back(t, carry):
        c = nc - 1 - t
        rows = pl.ds(pl.multiple_of(c * RET_CHUNK, RET_CHUNK), RET_CHUNK)
        sb_sc[c] = st_sc[...].astype(BF16)
        kb = (k_ref[rows, :].astype(F32) * kdec_b).astype(BF16)
        upd = lax.dot_general(kb, v_ref[rows, :], tdims, preferred_element_type=F32)
        st_sc[...] = st_sc[...] * chunk_dec + upd
        return carry

    lax.fori_loop(0, nc, sweep_back, 0, unroll=2)

    st_sc[...] = jnp.zeros_like(st_sc)

    def chunk_rows(c):
        return pl.ds(pl.multiple_of(c * RET_CHUNK, RET_CHUNK), RET_CHUNK)

    def mix(c):
        rows = chunk_rows(c)
        qb16 = q_ref[rows, :]
        kb16 = k_ref[rows, :]
        vc = v_ref[rows, :]
        qf32 = qb16.astype(F32)
        scores = lax.dot_general(qb16, kb16, (((1,), (1,)), ((), ())), preferred_element_type=F32)
        y = jnp.dot((scores * intra_dec).astype(BF16), vc, preferred_element_type=F32)
        y = y + jnp.dot((qf32 * qdec_f).astype(BF16), st_sc[...].astype(BF16),
                        preferred_element_type=F32)
        y = y + jnp.dot((qf32 * qdec_b).astype(BF16), sb_sc[c], preferred_element_type=F32)
        y_sc[...] = y
        kf = (kb16.astype(F32) * kdec_f).astype(BF16)
        upd = lax.dot_general(kf, vc, tdims, preferred_element_type=F32)
        st_sc[...] = st_sc[...] * chunk_dec + upd

    def finish(c):
        rows = chunk_rows(c)
        y = y_sc[...]
        mu = jnp.mean(y, axis=-1, keepdims=True)
        d = y - mu
        var = jnp.mean(d * d, axis=-1, keepdims=True)
        yn = (d * lax.rsqrt(var + EPS)) * gn_ref[...]
        gate = g_ref[rows, :].astype(F32)
        o_ref[rows, :] = (yn * jax.nn.silu(gate)).astype(BF16)

    mix(0)

    def sweep_fwd(c, carry):
        finish(c - 1)
        mix(c)
        return carry

    lax.fori_loop(1, nc, sweep_fwd, 0)
    finish(nc - 1)


def _retention(proj, logg, gn_g, bsz, seq):
    nc = seq // RET_CHUNK
    qb = COL_Q // RET_QK_DIM
    kb = COL_K // RET_QK_DIM
    vb = COL_V // RET_V_DIM
    gb = COL_G // RET_V_DIM
    return pl.pallas_call(
        functools.partial(_ret_kernel, nc=nc),
        grid=(bsz, RET_HEADS),
        in_specs=[
            pl.BlockSpec(memory_space=pltpu.SMEM),
            pl.BlockSpec((seq, RET_QK_DIM), lambda b, h: (b, qb + h)),
            pl.BlockSpec((seq, RET_QK_DIM), lambda b, h: (b, kb + h)),
            pl.BlockSpec((seq, RET_V_DIM), lambda b, h: (b, vb + h)),
            pl.BlockSpec((seq, RET_V_DIM), lambda b, h: (b, gb + h)),
            pl.BlockSpec((1, RET_V_DIM), lambda b, h: (0, h)),
        ],
        out_specs=pl.BlockSpec((seq, RET_V_DIM), lambda b, h: (b, h)),
        out_shape=jax.ShapeDtypeStruct((bsz * seq, RET_V), BF16),
        scratch_shapes=[
            pltpu.VMEM((nc, RET_QK_DIM, RET_V_DIM), BF16),
            pltpu.VMEM((RET_QK_DIM, RET_V_DIM), F32),
            pltpu.VMEM((RET_CHUNK, RET_V_DIM), F32),
        ],
        compiler_params=_cparams(("parallel", "arbitrary")),
        name="retention",
    )(logg, proj, proj, proj, proj, gn_g)


def _split_bf16(v):
    hi = v.astype(BF16)
    lo = (v - hi.astype(F32)).astype(BF16)
    return hi, lo


def _ssd_kernel(z_ref, x_ref, b_ref, c_ref, dt_ref, cwx_ref, cwb_ref, cwc_ref,
                cbx_ref, cbb_ref, cbc_ref, bias_ref, alog_ref, dskip_ref, ng_ref,
                o_ref, xs_sc, bm_sc, cm_sc, stb_sc, st_sc, sh_sc, e_sc,
                pcol_sc, prowt_sc, din_sc, wgt_sc, dtp_sc, y_sc, xw_sc, dch_sc, *, nc, pack):
    seq = nc * CHUNK
    hpg = SSM_HPG
    gw = SSM_GROUP_W
    cw = gw + 2 * SSM_STATE
    nblk = nc // pack
    tdims = (((0,), (0,)), ((), ()))

    def chunk_rows(c):
        return pl.ds(pl.multiple_of(c * CHUNK, CHUNK), CHUNK)

    si = lax.broadcasted_iota(jnp.int32, (CHUNK, CONV_K), 0)
    sr = lax.broadcasted_iota(jnp.int32, (CHUNK, CONV_K), 1)
    for idx, t in enumerate(CONV_SIDE_TAPS):
        sh_sc[idx] = jnp.where(sr == si + CONV_PAD + t - SSM_CONV // 2, 1.0, 0.0).astype(BF16)
    er = lax.broadcasted_iota(jnp.int32, (2 * LANES, 2 * gw), 0)
    ec = lax.broadcasted_iota(jnp.int32, (2 * LANES, 2 * gw), 1) // SSM_HEAD_DIM
    e_sc[...] = jnp.where(er - E_ORIGIN == ec, 1.0, 0.0).astype(BF16)

    def expand(v, slot, col0, ncol):
        e = e_sc[pl.ds(pl.multiple_of(E_ORIGIN - 2 * hpg * slot, 2 * hpg), LANES),
                 col0:col0 + ncol]
        return jnp.dot(v.astype(BF16), e, preferred_element_type=F32)

    w_all = jnp.concatenate([cwx_ref[...], cwb_ref[...], cwc_ref[...]], axis=1)
    b_all = jnp.concatenate([cbx_ref[...], cbb_ref[...], cbc_ref[...]], axis=1)
    zero16 = jnp.zeros((), BF16)

    def conv_body(c, carry):
        r0 = pl.multiple_of(c * CHUNK, CHUNK)
        prev0 = pl.multiple_of(jnp.maximum(r0 - CONV_PAD, 0), CONV_PAD)
        next0 = pl.multiple_of(jnp.minimum(r0 + CHUNK, seq - CONV_PAD), CONV_PAD)
        pieces = []
        for src_ref in (x_ref, b_ref, c_ref):
            width = src_ref.shape[1]
            pieces.append(jnp.concatenate([
                jnp.where(c > 0, src_ref[pl.ds(prev0, CONV_PAD), :], zero16),
                src_ref[pl.ds(r0, CHUNK), :],
                jnp.where(c < nc - 1, src_ref[pl.ds(next0, CONV_PAD), :], zero16),
                jnp.zeros((CONV_K - CHUNK - 2 * CONV_PAD, width), BF16)], axis=0))
        stage = jnp.concatenate(pieces, axis=1)
        mid = SSM_CONV // 2
        acc = stage[CONV_PAD:CONV_PAD + CHUNK, :].astype(F32) * w_all[mid:mid + 1, :]
        for idx, t in enumerate(CONV_SIDE_TAPS):
            acc = acc + jnp.dot(sh_sc[idx], stage, preferred_element_type=F32) * w_all[t:t + 1, :]
        out = jax.nn.silu(acc + b_all).astype(BF16)
        xs_sc[pl.ds(r0, CHUNK), :] = out[:, :gw]
        bm_sc[pl.ds(r0, CHUNK), :] = out[:, gw:gw + SSM_STATE]
        cm_sc[pl.ds(r0, CHUNK), :] = out[:, gw + SSM_STATE:]
        return carry

    lax.fori_loop(0, nc, conv_body, 0, unroll=2)

    lane = lax.broadcasted_iota(jnp.int32, (CHUNK, LANES), 1)
    fwd_lane = jnp.bitwise_and(lane, 2 * hpg - 1) < hpg
    first_slot = lane < 2 * hpg
    ii = lax.broadcasted_iota(jnp.int32, (CHUNK, CHUNK), 0)
    jj = lax.broadcasted_iota(jnp.int32, (CHUNK, CHUNK), 1)
    tril = jnp.where(jj <= ii, 1.0, 0.0).astype(F32)
    causal = jj <= ii
    diag = ii == jj
    pair_lo = lane < SSM_HEAD_DIM
    bias_row = bias_ref[0]
    a_row = -jnp.exp(alog_ref[0])

    def pre_body(blk, carry):
        raw = jnp.zeros((CHUNK, LANES), F32)
        for k in range(pack):
            t = jnp.where(first_slot, dt_ref[chunk_rows(blk * pack + k), :], 0.0)
            raw = raw + (pltpu.roll(t, 2 * hpg * k, axis=1) if k else t)
        dtp = jax.nn.softplus(raw + bias_row)
        la = dtp * a_row
        cum = jnp.dot(tril, la, preferred_element_type=F32, precision=lax.Precision.HIGHEST)
        tot = cum[CHUNK - 1:CHUNK, :]
        excl = cum - la
        pcol = jnp.where(fwd_lane, cum, tot - excl)
        pcol_sc[blk] = pcol
        prowt_sc[blk] = (pcol - jnp.log(dtp)).T
        din_sc[blk] = jnp.exp(pcol)
        wgt_sc[blk] = jnp.exp(jnp.where(fwd_lane, tot - cum, excl)) * dtp
        dtp_sc[blk] = dtp
        return carry

    lax.fori_loop(0, nblk, pre_body, 0)

    def stage_back(c, slot_buf):
        blk = c // pack
        slot = c - blk * pack
        lhs = jnp.concatenate([wgt_sc[blk], din_sc[blk, 0:8, :]], axis=0)
        ex = expand(lhs, slot, gw, gw)
        xw_sc[slot_buf] = (xs_sc[chunk_rows(c), :].astype(F32) * ex[:CHUNK, :]).astype(BF16)
        dch_sc[slot_buf] = ex[CHUNK:CHUNK + 8, :]

    st_sc[...] = jnp.zeros_like(st_sc)
    stage_back(nc - 1, 0)

    def sweep_back(t, carry):
        c = nc - 1 - t
        buf = jnp.bitwise_and(t, 1)
        stb_sc[c] = st_sc[...].astype(BF16)
        upd = lax.dot_general(bm_sc[chunk_rows(c), :], xw_sc[buf], tdims,
                              preferred_element_type=F32)
        st_sc[...] = st_sc[...] * dch_sc[buf][0:1, :] + upd
        stage_back(jnp.maximum(c - 1, 0), 1 - buf)
        return carry

    lax.fori_loop(0, nc, sweep_back, 0, unroll=2)

    def mix(c):
        rows = chunk_rows(c)
        blk = c // pack
        slot = c - blk * pack
        unpack = jnp.bitwise_and(LANES - 2 * hpg * slot, LANES - 1)
        pcol = pltpu.roll(pcol_sc[blk], unpack, axis=1)
        prow_t = prowt_sc[blk, pl.ds(pl.multiple_of(2 * hpg * slot, 2 * hpg), 2 * hpg), :]
        xs = xs_sc[rows, :]
        bm = bm_sc[rows, :]
        cm = cm_sc[rows, :]
        cb = lax.dot_general(cm, bm, (((1,), (1,)), ((), ())), preferred_element_type=F32)
        cb_diag = jnp.sum(jnp.where(diag, cb, 0.0), axis=1, keepdims=True)
        ue = expand(jnp.where(fwd_lane, wgt_sc[blk], cb_diag * dtp_sc[blk]), slot, 0, 2 * gw)
        de = expand(din_sc[blk], slot, 0, 2 * gw)
        y_st = (jnp.dot(cm, st_sc[...].astype(BF16), preferred_element_type=F32) * de[:, :gw]
                + jnp.dot(cm, stb_sc[c], preferred_element_type=F32) * de[:, gw:])
        pieces = []
        for pr in range(hpg // 2):
            xs_p = xs[:, pr * LANES:(pr + 1) * LANES]
            y_p = jnp.zeros((CHUNK, LANES), F32)
            for sub in range(2):
                hh = 2 * pr + sub
                keep = pair_lo if sub == 0 else jnp.logical_not(pair_lo)
                col_f = jnp.broadcast_to(pcol[:, hh:hh + 1], (CHUNK, CHUNK))
                col_b = jnp.broadcast_to(pcol[:, hpg + hh:hpg + hh + 1], (CHUNK, CHUNK))
                arg = jnp.where(causal, col_f - prow_t[hh:hh + 1, :],
                                col_b - prow_t[hpg + hh:hpg + hh + 1, :])
                m_h = (cb * jnp.exp(arg)).astype(BF16)
                y_p = y_p + jnp.dot(m_h, jnp.where(keep, xs_p, zero16),
                                    preferred_element_type=F32)
            pieces.append(y_p)
        xs32 = xs.astype(F32)
        y_sc[...] = jnp.concatenate(pieces, axis=1) + y_st + (dskip_ref[...] + ue[:, gw:]) * xs32
        xw = (xs32 * ue[:, :gw]).astype(BF16)
        upd = lax.dot_general(bm, xw, tdims, preferred_element_type=F32)
        st_sc[...] = st_sc[...] * de[CHUNK - 1:CHUNK, :gw] + upd

    def finish(c):
        rows = chunk_rows(c)
        y = y_sc[...] * jax.nn.silu(z_ref[rows, :].astype(F32))
        y = y * lax.rsqrt(jnp.mean(y * y, axis=-1, keepdims=True) + EPS)
        o_ref[rows, :] = (y * ng_ref[...]).astype(BF16)

    st_sc[...] = jnp.zeros_like(st_sc)
    mix(0)

    def sweep_fwd(c, carry):
        finish(c - 1)
        mix(c)
        return carry

    lax.fori_loop(1, nc, sweep_fwd, 0, unroll=2)
    finish(nc - 1)


def _ssd(proj, dt, cw_x, cw_b, cw_c, cb_x, cb_b, cb_c, bias_grp, alog_grp, dskip, norm_g, bsz, seq):
    nc = seq // CHUNK
    zb = COL_Z // SSM_GROUP_W
    xb = COL_X // SSM_GROUP_W
    bb = COL_B // SSM_STATE
    cb = COL_C // SSM_STATE
    gw = SSM_GROUP_W
    pack = min(LANES // PACK_LANES, nc)
    assert nc % pack == 0
    nblk = nc // pack
    packed = pltpu.VMEM((nblk, CHUNK, LANES), F32)
    return pl.pallas_call(
        functools.partial(_ssd_kernel, nc=nc, pack=pack),
        grid=(bsz, SSM_GROUPS),
        in_specs=[
            pl.BlockSpec((seq, gw), lambda b, g: (b, zb + g)),
            pl.BlockSpec((seq, gw), lambda b, g: (b, xb + g)),
            pl.BlockSpec((seq, SSM_STATE), lambda b, g: (b, bb + g)),
            pl.BlockSpec((seq, SSM_STATE), lambda b, g: (b, cb + g)),
            pl.BlockSpec((seq, LANES), lambda b, g: (b, g)),
            pl.BlockSpec((SSM_CONV, gw), lambda b, g: (0, g)),
            pl.BlockSpec((SSM_CONV, SSM_STATE), lambda b, g: (0, g)),
            pl.BlockSpec((SSM_CONV, SSM_STATE), lambda b, g: (0, g)),
            pl.BlockSpec((1, gw), lambda b, g: (0, g)),
            pl.BlockSpec((1, SSM_STATE), lambda b, g: (0, g)),
            pl.BlockSpec((1, SSM_STATE), lambda b, g: (0, g)),
            pl.BlockSpec((1, 1, LANES), lambda b, g: (g, 0, 0)),
            pl.BlockSpec((1, 1, LANES), lambda b, g: (g, 0, 0)),
            pl.BlockSpec((1, gw), lambda b, g: (0, g)),
            pl.BlockSpec((1, gw), lambda b, g: (0, g)),
        ],
        out_specs=pl.BlockSpec((seq, gw), lambda b, g: (b, g)),
        out_shape=jax.ShapeDtypeStruct((bsz * seq, SSM_INNER), BF16),
        scratch_shapes=[
            pltpu.VMEM((seq, gw), BF16),
            pltpu.VMEM((seq, SSM_STATE), BF16),
            pltpu.VMEM((seq, SSM_STATE), BF16),
            pltpu.VMEM((nc, SSM_STATE, gw), BF16),
            pltpu.VMEM((SSM_STATE, gw), F32),
            pltpu.VMEM((len(CONV_SIDE_TAPS), CHUNK, CONV_K), BF16),
            pltpu.VMEM((2 * LANES, 2 * gw), BF16),
            packed, packed, packed, packed, packed,
            pltpu.VMEM((CHUNK, gw), F32),
            pltpu.VMEM((2, CHUNK, gw), BF16),
            pltpu.VMEM((2, 8, gw), F32),
        ],
        compiler_params=_cparams(("parallel", "arbitrary")),
        name="ssd",
    )(proj, proj, proj, proj, dt, cw_x, cw_b, cw_c, cb_x, cb_b, cb_c, bias_grp, alog_grp,
      dskip, norm_g)


def _merge_kernel(yr_ref, ys_ref, gr_ref, gs_ref, x_ref, wr_ref, ws_ref, wo_ref, o_ref):
    y_ret = jnp.dot(yr_ref[...], wr_ref[...], preferred_element_type=F32)
    y_ssm = jnp.dot(ys_ref[...], ws_ref[...], preferred_element_type=F32)
    mixed = (jax.nn.sigmoid(gr_ref[...].astype(F32)) * y_ret
             + jax.nn.sigmoid(gs_ref[...].astype(F32)) * y_ssm)
    o_ref[...] = x_ref[...] + jnp.dot(mixed.astype(BF16), wo_ref[...], preferred_element_type=F32)


def _resident(shape):
    return pl.BlockSpec(shape, lambda i: (0,) * len(shape), pipeline_mode=pl.Buffered(1))


def _merge(yret, yssm, proj, x2, w_ret_o, w_ssm_o, w_out, tm):
    m = x2.shape[0]
    grb = COL_GATE_RET // D_MODEL
    gsb = COL_GATE_SSM // D_MODEL
    return pl.pallas_call(
        _merge_kernel,
        grid=(m // tm,),
        in_specs=[
            pl.BlockSpec((tm, RET_V), lambda i: (i, 0)),
            pl.BlockSpec((tm, SSM_INNER), lambda i: (i, 0)),
            pl.BlockSpec((tm, D_MODEL), lambda i: (i, grb)),
            pl.BlockSpec((tm, D_MODEL), lambda i: (i, gsb)),
            pl.BlockSpec((tm, D_MODEL), lambda i: (i, 0)),
            _resident((RET_V, D_MODEL)),
            _resident((SSM_INNER, D_MODEL)),
            _resident((D_MODEL, D_MODEL)),
        ],
        out_specs=pl.BlockSpec((tm, D_MODEL), lambda i: (i, 0)),
        out_shape=jax.ShapeDtypeStruct((m, D_MODEL), F32),
        compiler_params=_cparams(("parallel",)),
        name="merge",
    )(yret, yssm, proj, proj, x2, w_ret_o, w_ssm_o, w_out)


def _rms(x, g):
    return (x * lax.rsqrt(jnp.mean(x * x, axis=-1, keepdims=True) + EPS)) * g


def _mlp_kernel(x_ref, gm_ref, wu_ref, wd_ref, gf_ref, o_ref):
    x = x_ref[...]
    h = _rms(x, gm_ref[...]).astype(BF16)
    up = jnp.dot(h, wu_ref[...], preferred_element_type=F32)
    act = jnp.square(jnp.maximum(up, 0.0)).astype(BF16)
    x2 = x + jnp.dot(act, wd_ref[...], preferred_element_type=F32)
    o_ref[...] = _rms(x2, gf_ref[...])


def _mlp(x1, g_mlp, w_up, w_down, g_final, tm):
    m = x1.shape[0]
    return pl.pallas_call(
        _mlp_kernel,
        grid=(m // tm,),
        in_specs=[
            pl.BlockSpec((tm, D_MODEL), lambda i: (i, 0)),
            _resident((1, D_MODEL)),
            _resident((D_MODEL, D_FF)),
            _resident((D_FF, D_MODEL)),
            _resident((1, D_MODEL)),
        ],
        out_specs=pl.BlockSpec((tm, D_MODEL), lambda i: (i, 0)),
        out_shape=jax.ShapeDtypeStruct((m, D_MODEL), F32),
        compiler_params=_cparams(("parallel",)),
        name="mlp",
    )(x1, g_mlp, w_up, w_down, g_final)


def _group_rows(fwd, bwd):
    f = fwd.reshape(SSM_GROUPS, SSM_HPG)
    b = bwd.reshape(SSM_GROUPS, SSM_HPG)
    one = jnp.concatenate([f, b], axis=1)
    return jnp.tile(one, (1, LANES // PACK_LANES)).reshape(SSM_GROUPS, 1, LANES)


def _layer(x2, pos2, bsz, seq, norm_mix_g, w_in, ret_gn_g, w_ret_o, conv_w, conv_b, dt_bias_f,
           dt_bias_b, a_log_f, a_log_b, ssm_d, ssm_norm_g, w_ssm_o, w_out, norm_mlp_g, w_mlp_up,
           w_mlp_down, g_final):
    m = bsz * seq
    dt0 = COL_C + SSM_GROUPS * SSM_STATE
    w_main = jnp.concatenate([w_in[:, :dt0], w_in[:, dt0 + 2 * SSM_HEADS:]], axis=1).astype(BF16)
    wdt = w_in[:, dt0:dt0 + 2 * SSM_HEADS]
    wdt_f = wdt[:, :SSM_HEADS].reshape(D_MODEL, SSM_GROUPS, SSM_HPG)
    wdt_b = wdt[:, SSM_HEADS:].reshape(D_MODEL, SSM_GROUPS, SSM_HPG)
    wdt_pad = jnp.zeros((D_MODEL, SSM_GROUPS, LANES - 2 * SSM_HPG), F32)
    w_dt = jnp.concatenate([wdt_f, wdt_b, wdt_pad], axis=2).reshape(D_MODEL, SSM_GROUPS * LANES)
    w_dt = w_dt.astype(BF16)

    inv = (ROPE_BASE ** (-jnp.arange(ROPE_HALF, dtype=F32) / ROPE_HALF)).reshape(1, ROPE_HALF)
    tm1 = min(1024, m)
    proj, dt = _inproj(x2, pos2, inv, norm_mix_g.reshape(1, D_MODEL), w_main, w_dt, tm1)

    logg = jnp.log1p(-jnp.exp2(-5.0 - jnp.arange(RET_HEADS, dtype=F32)))
    yret = _retention(proj, logg, ret_gn_g.reshape(1, RET_V), bsz, seq)

    nbc = SSM_GROUPS * SSM_STATE
    cw_x, cw_b, cw_c = (conv_w[:, :SSM_INNER], conv_w[:, SSM_INNER:SSM_INNER + nbc],
                        conv_w[:, SSM_INNER + nbc:])
    cb2 = conv_b.reshape(1, -1)
    cb_x, cb_b, cb_c = (cb2[:, :SSM_INNER], cb2[:, SSM_INNER:SSM_INNER + nbc],
                        cb2[:, SSM_INNER + nbc:])
    dskip = jnp.repeat(ssm_d, SSM_HEAD_DIM).reshape(1, SSM_INNER)
    yssm = _ssd(proj, dt, cw_x, cw_b, cw_c, cb_x, cb_b, cb_c,
                _group_rows(dt_bias_f, dt_bias_b), _group_rows(a_log_f, a_log_b),
                dskip, ssm_norm_g.reshape(1, SSM_INNER), bsz, seq)

    tm4 = min(512, m)
    x1 = _merge(yret, yssm, proj, x2, w_ret_o.astype(BF16), w_ssm_o.astype(BF16),
                w_out.astype(BF16), tm4)
    return _mlp(x1, norm_mlp_g.reshape(1, D_MODEL), w_mlp_up.astype(BF16),
                w_mlp_down.astype(BF16), g_final, tm4)


def kernel(x, positions, norm_mix_g, w_in, ret_gn_g, w_ret_o, conv_w, conv_b, dt_bias_f, dt_bias_b,
           a_log_f, a_log_b, ssm_d, ssm_norm_g, w_ssm_o, w_out, norm_mlp_g, w_mlp_up, w_mlp_down,
           norm_final_g):
    bsz, seq, _ = x.shape
    depth = w_in.shape[0]
    assert depth == 1, "the final-norm fusion in the MLP call assumes a single layer"
    x2 = x.reshape(bsz * seq, D_MODEL)
    pos2 = positions.reshape(bsz * seq, 1)
    out = _layer(x2, pos2, bsz, seq, norm_mix_g[0], w_in[0], ret_gn_g[0], w_ret_o[0], conv_w[0],
                 conv_b[0], dt_bias_f[0], dt_bias_b[0], a_log_f[0], a_log_b[0], ssm_d[0],
                 ssm_norm_g[0], w_ssm_o[0], w_out[0], norm_mlp_g[0], w_mlp_up[0], w_mlp_down[0],
                 norm_final_g.reshape(1, D_MODEL))
    return out.reshape(bsz, seq, D_MODEL)
```

```python
import functools

import jax
import jax.numpy as jnp
from jax import lax
from jax.experimental import pallas as pl
from jax.experimental.pallas import tpu as pltpu

F32 = jnp.float32
BF16 = jnp.bfloat16

D_MODEL = 1024
RET_HEADS = 4
RET_QK_DIM = 256
RET_V_DIM = 512
RET_QK = RET_HEADS * RET_QK_DIM
RET_V = RET_HEADS * RET_V_DIM
ROPE_BASE = 10000.0
ROPE_HALF = RET_QK_DIM // 2
SSM_INNER = 2 * D_MODEL
SSM_HEAD_DIM = 64
SSM_HEADS = SSM_INNER // SSM_HEAD_DIM
SSM_GROUPS = 4
SSM_HPG = SSM_HEADS // SSM_GROUPS
SSM_STATE = 128
SSM_CONV = 5
SSM_GROUP_W = SSM_HPG * SSM_HEAD_DIM
D_FF = 4 * D_MODEL
EPS = 1e-6
CHUNK = 128
RET_CHUNK = 256

LANES = 128
BF16_ROWS = 16
MXU_K = 256
CONV_PAD = BF16_ROWS
CONV_K = MXU_K
CONV_SIDE_TAPS = tuple(t for t in range(SSM_CONV) if t != SSM_CONV // 2)
PACK_LANES = 2 * SSM_HPG
E_ORIGIN = LANES - PACK_LANES
LOG2E = 1.4426950408889634

COL_Q = 0
COL_K = COL_Q + RET_QK
COL_V = COL_K + RET_QK
COL_G = COL_V + RET_V
COL_Z = COL_G + RET_V
COL_X = COL_Z + SSM_INNER
COL_B = COL_X + SSM_INNER
COL_C = COL_B + SSM_GROUPS * SSM_STATE
COL_GATE_RET = COL_C + SSM_GROUPS * SSM_STATE
COL_GATE_SSM = COL_GATE_RET + D_MODEL
PROJ_W = COL_GATE_SSM + D_MODEL

VMEM_LIMIT = 56 * 1024 * 1024


def _cparams(sem):
    return pltpu.CompilerParams(dimension_semantics=sem, vmem_limit_bytes=VMEM_LIMIT)


def _sigmoid(x):
    return 0.5 * jnp.tanh(0.5 * x) + 0.5


def _silu(x):
    return x * _sigmoid(x)


def _rope_store(acc, c, s, scale, proj_ref):
    for hd in range(RET_HEADS):
        lo = hd * RET_QK_DIM
        t1 = acc[:, lo:lo + ROPE_HALF]
        t2 = acc[:, lo + ROPE_HALF:lo + RET_QK_DIM]
        proj_ref[:, lo:lo + ROPE_HALF] = ((t1 * c - t2 * s) * scale).astype(BF16)
        proj_ref[:, lo + ROPE_HALF:lo + RET_QK_DIM] = ((t1 * s + t2 * c) * scale).astype(BF16)


def _inproj_kernel(x_ref, pos_ref, inv_ref, g_ref, wa_ref, wb_ref, wdt_ref, proj_ref, dt_ref,
                   h_sc, cos_sc, sin_sc, *, n_a):
    j = pl.program_id(1)

    @pl.when(j == 0)
    def _():
        x = x_ref[...]
        ms = jnp.mean(x * x, axis=-1, keepdims=True)
        h = ((x * lax.rsqrt(ms + EPS)) * g_ref[...]).astype(BF16)
        h_sc[...] = h
        dt_ref[...] = jnp.dot(h, wdt_ref[...], preferred_element_type=F32)
        acc = jnp.dot(h, wa_ref[...], preferred_element_type=F32)
        ang = pos_ref[...].astype(F32) * inv_ref[...]
        c = jnp.cos(ang)
        s = jnp.sin(ang)
        cos_sc[...] = c
        sin_sc[...] = s
        _rope_store(acc, c, s, 1.0, proj_ref)

    @pl.when(j == 1)
    def _():
        acc = jnp.dot(h_sc[...], wa_ref[...], preferred_element_type=F32)
        _rope_store(acc, cos_sc[...], sin_sc[...], RET_QK_DIM ** -0.5, proj_ref)

    @pl.when(jnp.logical_and(j >= 2, j < n_a))
    def _():
        proj_ref[...] = jnp.dot(h_sc[...], wa_ref[...], preferred_element_type=F32).astype(BF16)

    @pl.when(j >= n_a)
    def _():
        proj_ref[...] = jnp.dot(h_sc[...], wb_ref[...], preferred_element_type=F32).astype(BF16)


def _inproj(x2, pos2, inv, g, w_a, w_b, w_dt, tm):
    m = x2.shape[0]
    tn = RET_QK
    n_a = w_a.shape[1] // tn
    n_b = w_b.shape[1] // tn
    assert (n_a + n_b) * tn == PROJ_W and n_a >= 2
    return pl.pallas_call(
        functools.partial(_inproj_kernel, n_a=n_a),
        grid=(m // tm, n_a + n_b),
        in_specs=[
            pl.BlockSpec((tm, D_MODEL), lambda i, j: (i, 0)),
            pl.BlockSpec((tm, 1), lambda i, j: (i, 0)),
            pl.BlockSpec((1, ROPE_HALF), lambda i, j: (0, 0)),
            pl.BlockSpec((1, D_MODEL), lambda i, j: (0, 0)),
            pl.BlockSpec((D_MODEL, tn), lambda i, j: (0, jnp.minimum(j, n_a - 1))),
            pl.BlockSpec((D_MODEL, tn), lambda i, j: (0, jnp.maximum(j - n_a, 0))),
            pl.BlockSpec((D_MODEL, SSM_GROUPS * LANES), lambda i, j: (0, 0)),
        ],
        out_specs=[
            pl.BlockSpec((tm, tn), lambda i, j: (i, j)),
            pl.BlockSpec((tm, SSM_GROUPS * LANES), lambda i, j: (i, 0)),
        ],
        out_shape=[
            jax.ShapeDtypeStruct((m, PROJ_W), BF16),
            jax.ShapeDtypeStruct((m, SSM_GROUPS * LANES), F32),
        ],
        scratch_shapes=[
            pltpu.VMEM((tm, D_MODEL), BF16),
            pltpu.VMEM((tm, ROPE_HALF), F32),
            pltpu.VMEM((tm, ROPE_HALF), F32),
        ],
        compiler_params=_cparams(("parallel", "arbitrary")),
        name="inproj",
    )(x2, pos2, inv, g, w_a, w_b, w_dt)


def _ret_kernel(logg_ref, q_ref, k_ref, v_ref, g_ref, gn_ref, o_ref, sb_sc, st_sc, y_sc, *, nc):
    lg = logg_ref[pl.program_id(1)]
    pos = lax.broadcasted_iota(jnp.int32, (RET_CHUNK, RET_QK_DIM), 0).astype(F32)
    qdec_f = jnp.exp((pos + 1.0) * lg)
    kdec_f = jnp.exp((RET_CHUNK - 1.0 - pos) * lg)
    qdec_b = jnp.exp((RET_CHUNK - pos) * lg)
    kdec_b = jnp.exp(pos * lg)
    chunk_dec = jnp.exp(jnp.full((1, RET_V_DIM), float(RET_CHUNK), F32) * lg)
    ii = lax.broadcasted_iota(jnp.int32, (RET_CHUNK, RET_CHUNK), 0)
    jj = lax.broadcasted_iota(jnp.int32, (RET_CHUNK, RET_CHUNK), 1)
    intra_dec = jnp.exp(jnp.abs(ii - jj).astype(F32) * lg)
    tdims = (((0,), (0,)), ((), ()))

    st_sc[...] = jnp.zeros_like(st_sc)

    def sweep_back(t, carry):
        c = nc - 1 - t
        rows = pl.ds(pl.multiple_of(c * RET_CHUNK, RET_CHUNK), RET_CHUNK)
        sb_sc[c] = st_sc[...].astype(BF16)
        kb = (k_ref[rows, :].astype(F32) * kdec_b).astype(BF16)
        upd = lax.dot_general(kb, v_ref[rows, :], tdims, preferred_element_type=F32)
        st_sc[...] = st_sc[...] * chunk_dec + upd
        return carry

    lax.fori_loop(0, nc, sweep_back, 0, unroll=2)

    st_sc[...] = jnp.zeros_like(st_sc)

    def chunk_rows(c):
        return pl.ds(pl.multiple_of(c * RET_CHUNK, RET_CHUNK), RET_CHUNK)

    def mix(c):
        rows = chunk_rows(c)
        qb16 = q_ref[rows, :]
        kb16 = k_ref[rows, :]
        vc = v_ref[rows, :]
        qf32 = qb16.astype(F32)
        scores = lax.dot_general(qb16, kb16, (((1,), (1,)), ((), ())), preferred_element_type=F32)
        y = jnp.dot((scores * intra_dec).astype(BF16), vc, preferred_element_type=F32)
        y = y + jnp.dot((qf32 * qdec_f).astype(BF16), st_sc[...].astype(BF16),
                        preferred_element_type=F32)
        y = y + jnp.dot((qf32 * qdec_b).astype(BF16), sb_sc[c], preferred_element_type=F32)
        y_sc[...] = y
        kf = (kb16.astype(F32) * kdec_f).astype(BF16)
        upd = lax.dot_general(kf, vc, tdims, preferred_element_type=F32)
        st_sc[...] = st_sc[...] * chunk_dec + upd

    def finish(c):
        rows = chunk_rows(c)
        y = y_sc[...]
        mu = jnp.mean(y, axis=-1, keepdims=True)
        d = y - mu
        var = jnp.mean(d * d, axis=-1, keepdims=True)
        yn = (d * lax.rsqrt(var + EPS)) * gn_ref[...]
        gate = g_ref[rows, :].astype(F32)
        o_ref[rows, :] = (yn * _silu(gate)).astype(BF16)

    mix(0)

    def sweep_fwd(c, carry):
        finish(c - 1)
        mix(c)
        return carry

    lax.fori_loop(1, nc, sweep_fwd, 0, unroll=2)
    finish(nc - 1)


def _retention(proj, logg, gn_g, bsz, seq):
    nc = seq // RET_CHUNK
    qb = COL_Q // RET_QK_DIM
    kb = COL_K // RET_QK_DIM
    vb = COL_V // RET_V_DIM
    gb = COL_G // RET_V_DIM
    return pl.pallas_call(
        functools.partial(_ret_kernel, nc=nc),
        grid=(bsz, RET_HEADS),
        in_specs=[
            pl.BlockSpec(memory_space=pltpu.SMEM),
            pl.BlockSpec((seq, RET_QK_DIM), lambda b, h: (b, qb + h)),
            pl.BlockSpec((seq, RET_QK_DIM), lambda b, h: (b, kb + h)),
            pl.BlockSpec((seq, RET_V_DIM), lambda b, h: (b, vb + h)),
            pl.BlockSpec((seq, RET_V_DIM), lambda b, h: (b, gb + h)),
            pl.BlockSpec((1, RET_V_DIM), lambda b, h: (0, h)),
        ],
        out_specs=pl.BlockSpec((seq, RET_V_DIM), lambda b, h: (b, h)),
        out_shape=jax.ShapeDtypeStruct((bsz * seq, RET_V), BF16),
        scratch_shapes=[
            pltpu.VMEM((nc, RET_QK_DIM, RET_V_DIM), BF16),
            pltpu.VMEM((RET_QK_DIM, RET_V_DIM), F32),
            pltpu.VMEM((RET_CHUNK, RET_V_DIM), F32),
        ],
        compiler_params=_cparams(("parallel", "arbitrary")),
        name="retention",
    )(logg, proj, proj, proj, proj, gn_g)


def _split_bf16(v):
    hi = v.astype(BF16)
    lo = (v - hi.astype(F32)).astype(BF16)
    return hi, lo


def _ssd_kernel(z_ref, x_ref, b_ref, c_ref, dt_ref, cwx_ref, cwb_ref, cwc_ref,
                cbx_ref, cbb_ref, cbc_ref, bias_ref, alog_ref, dskip_ref, ng_ref,
                o_ref, xs_sc, bm_sc, cm_sc, stb_sc, st_sc, sh_sc, e_sc,
                pcol_sc, prowt_sc, din_sc, wgt_sc, dtp_sc, y_sc, xw_sc, dch_sc, *, nc, pack):
    seq = nc * CHUNK
    hpg = SSM_HPG
    gw = SSM_GROUP_W
    cw = gw + 2 * SSM_STATE
    nblk = nc // pack
    tdims = (((0,), (0,)), ((), ()))

    def chunk_rows(c):
        return pl.ds(pl.multiple_of(c * CHUNK, CHUNK), CHUNK)

    si = lax.broadcasted_iota(jnp.int32, (CHUNK, CONV_K), 0)
    sr = lax.broadcasted_iota(jnp.int32, (CHUNK, CONV_K), 1)
    for idx, t in enumerate(CONV_SIDE_TAPS):
        sh_sc[idx] = jnp.where(sr == si + CONV_PAD + t - SSM_CONV // 2, 1.0, 0.0).astype(BF16)
    er = lax.broadcasted_iota(jnp.int32, (2 * LANES, 2 * gw), 0)
    ec = lax.broadcasted_iota(jnp.int32, (2 * LANES, 2 * gw), 1) // SSM_HEAD_DIM
    e_sc[...] = jnp.where(er - E_ORIGIN == ec, 1.0, 0.0).astype(BF16)

    def expand(v, slot, col0, ncol):
        e = e_sc[pl.ds(pl.multiple_of(E_ORIGIN - 2 * hpg * slot, 2 * hpg), LANES),
                 col0:col0 + ncol]
        return jnp.dot(v.astype(BF16), e, preferred_element_type=F32)

    w_all = jnp.concatenate([cwx_ref[...], cwb_ref[...], cwc_ref[...]], axis=1)
    b_all = jnp.concatenate([cbx_ref[...], cbb_ref[...], cbc_ref[...]], axis=1)
    zero16 = jnp.zeros((), BF16)

    def conv_body(c, carry):
        r0 = pl.multiple_of(c * CHUNK, CHUNK)
        prev0 = pl.multiple_of(jnp.maximum(r0 - CONV_PAD, 0), CONV_PAD)
        next0 = pl.multiple_of(jnp.minimum(r0 + CHUNK, seq - CONV_PAD), CONV_PAD)
        pieces = []
        for src_ref in (x_ref, b_ref, c_ref):
            width = src_ref.shape[1]
            pieces.append(jnp.concatenate([
                jnp.where(c > 0, src_ref[pl.ds(prev0, CONV_PAD), :], zero16),
                src_ref[pl.ds(r0, CHUNK), :],
                jnp.where(c < nc - 1, src_ref[pl.ds(next0, CONV_PAD), :], zero16),
                jnp.zeros((CONV_K - CHUNK - 2 * CONV_PAD, width), BF16)], axis=0))
        stage = jnp.concatenate(pieces, axis=1)
        mid = SSM_CONV // 2
        acc = stage[CONV_PAD:CONV_PAD + CHUNK, :].astype(F32) * w_all[mid:mid + 1, :]
        for idx, t in enumerate(CONV_SIDE_TAPS):
            acc = acc + jnp.dot(sh_sc[idx], stage, preferred_element_type=F32) * w_all[t:t + 1, :]
        out = _silu(acc + b_all).astype(BF16)
        xs_sc[pl.ds(r0, CHUNK), :] = out[:, :gw]
        bm_sc[pl.ds(r0, CHUNK), :] = out[:, gw:gw + SSM_STATE]
        cm_sc[pl.ds(r0, CHUNK), :] = out[:, gw + SSM_STATE:]
        return carry

    lax.fori_loop(0, nc, conv_body, 0, unroll=2)

    lane = lax.broadcasted_iota(jnp.int32, (CHUNK, LANES), 1)
    fwd_lane = jnp.bitwise_and(lane, 2 * hpg - 1) < hpg
    first_slot = lane < 2 * hpg
    ii = lax.broadcasted_iota(jnp.int32, (CHUNK, CHUNK), 0)
    jj = lax.broadcasted_iota(jnp.int32, (CHUNK, CHUNK), 1)
    tril = jnp.where(jj <= ii, 1.0, 0.0).astype(F32)
    causal = jj <= ii
    diag = ii == jj
    pair_lo = lane < SSM_HEAD_DIM
    bias_row = bias_ref[0]
    a_row = -jnp.exp(alog_ref[0])

    def pre_body(blk, carry):
        raw = jnp.zeros((CHUNK, LANES), F32)
        for k in range(pack):
            t = jnp.where(first_slot, dt_ref[chunk_rows(blk * pack + k), :], 0.0)
            raw = raw + (pltpu.roll(t, 2 * hpg * k, axis=1) if k else t)
        dtp = jax.nn.softplus(raw + bias_row)
        la = dtp * a_row
        cum = jnp.dot(tril, la, preferred_element_type=F32, precision=lax.Precision.HIGHEST)
        tot = cum[CHUNK - 1:CHUNK, :]
        excl = cum - la
        pcol = jnp.where(fwd_lane, cum, tot - excl)
        pcol_sc[blk] = pcol * LOG2E
        prowt_sc[blk] = ((pcol - jnp.log(dtp)) * LOG2E).T
        din_sc[blk] = jnp.exp(pcol)
        wgt_sc[blk] = jnp.exp(jnp.where(fwd_lane, tot - cum, excl)) * dtp
        dtp_sc[blk] = dtp
        return carry

    lax.fori_loop(0, nblk, pre_body, 0)

    def stage_back(c, slot_buf):
        blk = c // pack
        slot = c - blk * pack
        lhs = jnp.concatenate([wgt_sc[blk], din_sc[blk, 0:8, :]], axis=0)
        ex = expand(lhs, slot, gw, gw)
        xw_sc[slot_buf] = (xs_sc[chunk_rows(c), :].astype(F32) * ex[:CHUNK, :]).astype(BF16)
        dch_sc[slot_buf] = ex[CHUNK:CHUNK + 8, :]

    st_sc[...] = jnp.zeros_like(st_sc)
    stage_back(nc - 1, 0)

    def sweep_back(t, carry):
        c = nc - 1 - t
        buf = jnp.bitwise_and(t, 1)
        stb_sc[c] = st_sc[...].astype(BF16)
        upd = lax.dot_general(bm_sc[chunk_rows(c), :], xw_sc[buf], tdims,
                              preferred_element_type=F32)
        st_sc[...] = st_sc[...] * dch_sc[buf][0:1, :] + upd
        stage_back(jnp.maximum(c - 1, 0), 1 - buf)
        return carry

    lax.fori_loop(0, nc, sweep_back, 0, unroll=2)

    def mix(c):
        rows = chunk_rows(c)
        blk = c // pack
        slot = c - blk * pack
        unpack = jnp.bitwise_and(LANES - 2 * hpg * slot, LANES - 1)
        pcol = pltpu.roll(pcol_sc[blk], unpack, axis=1)
        prow_t = prowt_sc[blk, pl.ds(pl.multiple_of(2 * hpg * slot, 2 * hpg), 2 * hpg), :]
        xs = xs_sc[rows, :]
        bm = bm_sc[rows, :]
        cm = cm_sc[rows, :]
        cb = lax.dot_general(cm, bm, (((1,), (1,)), ((), ())), preferred_element_type=F32)
        cb_diag = jnp.sum(jnp.where(diag, cb, 0.0), axis=1, keepdims=True)
        ue = expand(jnp.where(fwd_lane, wgt_sc[blk], cb_diag * dtp_sc[blk]), slot, 0, 2 * gw)
        de = expand(din_sc[blk], slot, 0, 2 * gw)
        y_st = (jnp.dot(cm, st_sc[...].astype(BF16), preferred_element_type=F32) * de[:, :gw]
                + jnp.dot(cm, stb_sc[c], preferred_element_type=F32) * de[:, gw:])
        pieces = []
        for pr in range(hpg // 2):
            xs_p = xs[:, pr * LANES:(pr + 1) * LANES]
            m_pair = []
            for hh in (2 * pr, 2 * pr + 1):
                col_f = jnp.broadcast_to(pcol[:, hh:hh + 1], (CHUNK, CHUNK))
                col_b = jnp.broadcast_to(pcol[:, hpg + hh:hpg + hh + 1], (CHUNK, CHUNK))
                arg = jnp.where(causal, col_f - prow_t[hh:hh + 1, :],
                                col_b - prow_t[hpg + hh:hpg + hh + 1, :])
                m_pair.append((cb * jnp.exp2(arg)).astype(BF16))
            x_pair = jnp.concatenate([jnp.where(pair_lo, xs_p, zero16),
                                      jnp.where(pair_lo, zero16, xs_p)], axis=0)
            pieces.append(jnp.dot(jnp.concatenate(m_pair, axis=1), x_pair,
                                  preferred_element_type=F32))
        xs32 = xs.astype(F32)
        y_sc[...] = jnp.concatenate(pieces, axis=1) + y_st + (dskip_ref[...] + ue[:, gw:]) * xs32
        xw = (xs32 * ue[:, :gw]).astype(BF16)
        upd = lax.dot_general(bm, xw, tdims, preferred_element_type=F32)
        st_sc[...] = st_sc[...] * de[CHUNK - 1:CHUNK, :gw] + upd

    def finish(c):
        rows = chunk_rows(c)
        y = y_sc[...] * _silu(z_ref[rows, :].astype(F32))
        y = y * lax.rsqrt(jnp.mean(y * y, axis=-1, keepdims=True) + EPS)
        o_ref[rows, :] = (y * ng_ref[...]).astype(BF16)

    st_sc[...] = jnp.zeros_like(st_sc)
    mix(0)

    def sweep_fwd(c, carry):
        finish(c - 1)
        mix(c)
        return carry

    lax.fori_loop(1, nc, sweep_fwd, 0, unroll=2)
    finish(nc - 1)


def _ssd(proj, dt, cw_x, cw_b, cw_c, cb_x, cb_b, cb_c, bias_grp, alog_grp, dskip, norm_g, bsz, seq):
    nc = seq // CHUNK
    zb = COL_Z // SSM_GROUP_W
    xb = COL_X // SSM_GROUP_W
    bb = COL_B // SSM_STATE
    cb = COL_C // SSM_STATE
    gw = SSM_GROUP_W
    pack = min(LANES // PACK_LANES, nc)
    assert nc % pack == 0
    nblk = nc // pack
    packed = pltpu.VMEM((nblk, CHUNK, LANES), F32)
    return pl.pallas_call(
        functools.partial(_ssd_kernel, nc=nc, pack=pack),
        grid=(bsz, SSM_GROUPS),
        in_specs=[
            pl.BlockSpec((seq, gw), lambda b, g: (b, zb + g)),
            pl.BlockSpec((seq, gw), lambda b, g: (b, xb + g)),
            pl.BlockSpec((seq, SSM_STATE), lambda b, g: (b, bb + g)),
            pl.BlockSpec((seq, SSM_STATE), lambda b, g: (b, cb + g)),
            pl.BlockSpec((seq, LANES), lambda b, g: (b, g)),
            pl.BlockSpec((SSM_CONV, gw), lambda b, g: (0, g)),
            pl.BlockSpec((SSM_CONV, SSM_STATE), lambda b, g: (0, g)),
            pl.BlockSpec((SSM_CONV, SSM_STATE), lambda b, g: (0, g)),
            pl.BlockSpec((1, gw), lambda b, g: (0, g)),
            pl.BlockSpec((1, SSM_STATE), lambda b, g: (0, g)),
            pl.BlockSpec((1, SSM_STATE), lambda b, g: (0, g)),
            pl.BlockSpec((1, 1, LANES), lambda b, g: (g, 0, 0)),
            pl.BlockSpec((1, 1, LANES), lambda b, g: (g, 0, 0)),
            pl.BlockSpec((1, gw), lambda b, g: (0, g)),
            pl.BlockSpec((1, gw), lambda b, g: (0, g)),
        ],
        out_specs=pl.BlockSpec((seq, gw), lambda b, g: (b, g)),
        out_shape=jax.ShapeDtypeStruct((bsz * seq, SSM_INNER), BF16),
        scratch_shapes=[
            pltpu.VMEM((seq, gw), BF16),
            pltpu.VMEM((seq, SSM_STATE), BF16),
            pltpu.VMEM((seq, SSM_STATE), BF16),
            pltpu.VMEM((nc, SSM_STATE, gw), BF16),
            pltpu.VMEM((SSM_STATE, gw), F32),
            pltpu.VMEM((len(CONV_SIDE_TAPS), CHUNK, CONV_K), BF16),
            pltpu.VMEM((2 * LANES, 2 * gw), BF16),
            packed, packed, packed, packed, packed,
            pltpu.VMEM((CHUNK, gw), F32),
            pltpu.VMEM((2, CHUNK, gw), BF16),
            pltpu.VMEM((2, 8, gw), F32),
        ],
        compiler_params=_cparams(("parallel", "arbitrary")),
        name="ssd",
    )(proj, proj, proj, proj, dt, cw_x, cw_b, cw_c, cb_x, cb_b, cb_c, bias_grp, alog_grp,
      dskip, norm_g)


def _merge_kernel(yr_ref, ys_ref, gr_ref, gs_ref, x_ref, wr_ref, ws_ref, wo_ref, o_ref):
    y_ret = jnp.dot(yr_ref[...], wr_ref[...], preferred_element_type=F32)
    y_ssm = jnp.dot(ys_ref[...], ws_ref[...], preferred_element_type=F32)
    mixed = (_sigmoid(gr_ref[...].astype(F32)) * y_ret
             + _sigmoid(gs_ref[...].astype(F32)) * y_ssm)
    o_ref[...] = x_ref[...] + jnp.dot(mixed.astype(BF16), wo_ref[...], preferred_element_type=F32)


def _resident(shape):
    return pl.BlockSpec(shape, lambda i: (0,) * len(shape), pipeline_mode=pl.Buffered(1))


def _merge(yret, yssm, proj, x2, w_ret_o, w_ssm_o, w_out, tm):
    m = x2.shape[0]
    grb = COL_GATE_RET // D_MODEL
    gsb = COL_GATE_SSM // D_MODEL
    return pl.pallas_call(
        _merge_kernel,
        grid=(m // tm,),
        in_specs=[
            pl.BlockSpec((tm, RET_V), lambda i: (i, 0)),
            pl.BlockSpec((tm, SSM_INNER), lambda i: (i, 0)),
            pl.BlockSpec((tm, D_MODEL), lambda i: (i, grb)),
            pl.BlockSpec((tm, D_MODEL), lambda i: (i, gsb)),
            pl.BlockSpec((tm, D_MODEL), lambda i: (i, 0)),
            _resident((RET_V, D_MODEL)),
            _resident((SSM_INNER, D_MODEL)),
            _resident((D_MODEL, D_MODEL)),
        ],
        out_specs=pl.BlockSpec((tm, D_MODEL), lambda i: (i, 0)),
        out_shape=jax.ShapeDtypeStruct((m, D_MODEL), F32),
        compiler_params=_cparams(("parallel",)),
        name="merge",
    )(yret, yssm, proj, proj, x2, w_ret_o, w_ssm_o, w_out)


def _rms(x, g):
    return (x * lax.rsqrt(jnp.mean(x * x, axis=-1, keepdims=True) + EPS)) * g


def _mlp_kernel(x_ref, gm_ref, wu_ref, wd_ref, gf_ref, o_ref):
    x = x_ref[...]
    h = _rms(x, gm_ref[...]).astype(BF16)
    up = jnp.dot(h, wu_ref[...], preferred_element_type=F32)
    act = jnp.square(jnp.maximum(up, 0.0)).astype(BF16)
    x2 = x + jnp.dot(act, wd_ref[...], preferred_element_type=F32)
    o_ref[...] = _rms(x2, gf_ref[...])


def _mlp(x1, g_mlp, w_up, w_down, g_final, tm):
    m = x1.shape[0]
    return pl.pallas_call(
        _mlp_kernel,
        grid=(m // tm,),
        in_specs=[
            pl.BlockSpec((tm, D_MODEL), lambda i: (i, 0)),
            _resident((1, D_MODEL)),
            _resident((D_MODEL, D_FF)),
            _resident((D_FF, D_MODEL)),
            _resident((1, D_MODEL)),
        ],
        out_specs=pl.BlockSpec((tm, D_MODEL), lambda i: (i, 0)),
        out_shape=jax.ShapeDtypeStruct((m, D_MODEL), F32),
        compiler_params=_cparams(("parallel",)),
        name="mlp",
    )(x1, g_mlp, w_up, w_down, g_final)


def _group_rows(fwd, bwd):
    f = fwd.reshape(SSM_GROUPS, SSM_HPG)
    b = bwd.reshape(SSM_GROUPS, SSM_HPG)
    one = jnp.concatenate([f, b], axis=1)
    return jnp.tile(one, (1, LANES // PACK_LANES)).reshape(SSM_GROUPS, 1, LANES)


def _layer(x2, pos2, bsz, seq, norm_mix_g, w_in, ret_gn_g, w_ret_o, conv_w, conv_b, dt_bias_f,
           dt_bias_b, a_log_f, a_log_b, ssm_d, ssm_norm_g, w_ssm_o, w_out, norm_mlp_g, w_mlp_up,
           w_mlp_down, g_final):
    m = bsz * seq
    dt0 = COL_C + SSM_GROUPS * SSM_STATE
    w_a = w_in[:, :dt0].astype(BF16)
    w_b = w_in[:, dt0 + 2 * SSM_HEADS:].astype(BF16)
    wdt = w_in[:, dt0:dt0 + 2 * SSM_HEADS]
    wdt_f = wdt[:, :SSM_HEADS].reshape(D_MODEL, SSM_GROUPS, SSM_HPG)
    wdt_b = wdt[:, SSM_HEADS:].reshape(D_MODEL, SSM_GROUPS, SSM_HPG)
    wdt_pad = jnp.zeros((D_MODEL, SSM_GROUPS, LANES - 2 * SSM_HPG), F32)
    w_dt = jnp.concatenate([wdt_f, wdt_b, wdt_pad], axis=2).reshape(D_MODEL, SSM_GROUPS * LANES)
    w_dt = w_dt.astype(BF16)

    inv = (ROPE_BASE ** (-jnp.arange(ROPE_HALF, dtype=F32) / ROPE_HALF)).reshape(1, ROPE_HALF)
    tm1 = min(1024, m)
    proj, dt = _inproj(x2, pos2, inv, norm_mix_g.reshape(1, D_MODEL), w_a, w_b, w_dt, tm1)

    logg = jnp.log1p(-jnp.exp2(-5.0 - jnp.arange(RET_HEADS, dtype=F32)))
    yret = _retention(proj, logg, ret_gn_g.reshape(1, RET_V), bsz, seq)

    nbc = SSM_GROUPS * SSM_STATE
    cw_x, cw_b, cw_c = (conv_w[:, :SSM_INNER], conv_w[:, SSM_INNER:SSM_INNER + nbc],
                        conv_w[:, SSM_INNER + nbc:])
    cb2 = conv_b.reshape(1, -1)
    cb_x, cb_b, cb_c = (cb2[:, :SSM_INNER], cb2[:, SSM_INNER:SSM_INNER + nbc],
                        cb2[:, SSM_INNER + nbc:])
    dskip = jnp.repeat(ssm_d, SSM_HEAD_DIM).reshape(1, SSM_INNER)
    yssm = _ssd(proj, dt, cw_x, cw_b, cw_c, cb_x, cb_b, cb_c,
                _group_rows(dt_bias_f, dt_bias_b), _group_rows(a_log_f, a_log_b),
                dskip, ssm_norm_g.reshape(1, SSM_INNER), bsz, seq)

    tm4 = min(512, m)
    x1 = _merge(yret, yssm, proj, x2, w_ret_o.astype(BF16), w_ssm_o.astype(BF16),
                w_out.astype(BF16), tm4)
    return _mlp(x1, norm_mlp_g.reshape(1, D_MODEL), w_mlp_up.astype(BF16),
                w_mlp_down.astype(BF16), g_final, tm4)


def kernel(x, positions, norm_mix_g, w_in, ret_gn_g, w_ret_o, conv_w, conv_b, dt_bias_f, dt_bias_b,
           a_log_f, a_log_b, ssm_d, ssm_norm_g, w_ssm_o, w_out, norm_mlp_g, w_mlp_up, w_mlp_down,
           norm_final_g):
    bsz, seq, _ = x.shape
    depth = w_in.shape[0]
    assert depth == 1, "the final-norm fusion in the MLP call assumes a single layer"
    x2 = x.reshape(bsz * seq, D_MODEL)
    pos2 = positions.reshape(bsz * seq, 1)
    out = _layer(x2, pos2, bsz, seq, norm_mix_g[0], w_in[0], ret_gn_g[0], w_ret_o[0], conv_w[0],
                 conv_b[0], dt_bias_f[0], dt_bias_b[0], a_log_f[0], a_log_b[0], ssm_d[0],
                 ssm_norm_g[0], w_ssm_o[0], w_out[0], norm_mlp_g[0], w_mlp_up[0], w_mlp_down[0],
                 norm_final_g.reshape(1, D_MODEL))
    return out.reshape(bsz, seq, D_MODEL)
```

```python
import functools

import jax
import jax.numpy as jnp
from jax import lax
from jax.experimental import pallas as pl
from jax.experimental.pallas import tpu as pltpu

F32 = jnp.float32
BF16 = jnp.bfloat16

D_MODEL = 1024
RET_HEADS = 4
RET_QK_DIM = 256
RET_V_DIM = 512
RET_QK = RET_HEADS * RET_QK_DIM
RET_V = RET_HEADS * RET_V_DIM
ROPE_BASE = 10000.0
ROPE_HALF = RET_QK_DIM // 2
SSM_INNER = 2 * D_MODEL
SSM_HEAD_DIM = 64
SSM_HEADS = SSM_INNER // SSM_HEAD_DIM
SSM_GROUPS = 4
SSM_HPG = SSM_HEADS // SSM_GROUPS
SSM_STATE = 128
SSM_CONV = 5
SSM_GROUP_W = SSM_HPG * SSM_HEAD_DIM
D_FF = 4 * D_MODEL
EPS = 1e-6
CHUNK = 128
RET_CHUNK = 256

LANES = 128
BF16_ROWS = 16
MXU_K = 256
CONV_PAD = BF16_ROWS
CONV_K = MXU_K
CONV_SIDE_TAPS = tuple(t for t in range(SSM_CONV) if t != SSM_CONV // 2)
PACK_LANES = 2 * SSM_HPG
E_ORIGIN = LANES - PACK_LANES
LOG2E = 1.4426950408889634

COL_Q = 0
COL_K = COL_Q + RET_QK
COL_V = COL_K + RET_QK
COL_G = COL_V + RET_V
COL_Z = COL_G + RET_V
COL_X = COL_Z + SSM_INNER
COL_B = COL_X + SSM_INNER
COL_C = COL_B + SSM_GROUPS * SSM_STATE
COL_GATE_RET = COL_C + SSM_GROUPS * SSM_STATE
COL_GATE_SSM = COL_GATE_RET + D_MODEL
PROJ_W = COL_GATE_SSM + D_MODEL
INPROJ_TN = 2 * RET_QK

VMEM_LIMIT = 56 * 1024 * 1024


def _cparams(sem):
    return pltpu.CompilerParams(dimension_semantics=sem, vmem_limit_bytes=VMEM_LIMIT)


def _sigmoid(x):
    return 0.5 * jnp.tanh(0.5 * x) + 0.5


def _silu(x):
    return x * _sigmoid(x)


def _rope_store(acc, c, s, scale, proj_ref, col0):
    for hd in range(RET_HEADS):
        lo = hd * RET_QK_DIM
        t1 = acc[:, lo:lo + ROPE_HALF]
        t2 = acc[:, lo + ROPE_HALF:lo + RET_QK_DIM]
        o = col0 + lo
        proj_ref[:, o:o + ROPE_HALF] = ((t1 * c - t2 * s) * scale).astype(BF16)
        proj_ref[:, o + ROPE_HALF:o + RET_QK_DIM] = ((t1 * s + t2 * c) * scale).astype(BF16)


def _inproj_kernel(x_ref, pos_ref, inv_ref, g_ref, w_ref, wdt_ref, proj_ref, dt_ref, h_sc,
                   *, n_full, tail):
    j = pl.program_id(1)

    @pl.when(j == 0)
    def _():
        x = x_ref[...]
        ms = jnp.mean(x * x, axis=-1, keepdims=True)
        h = ((x * lax.rsqrt(ms + EPS)) * g_ref[...]).astype(BF16)
        h_sc[...] = h
        dt_ref[...] = jnp.dot(h, wdt_ref[...], preferred_element_type=F32)
        ang = pos_ref[...].astype(F32) * inv_ref[...]
        c = jnp.cos(ang)
        s = jnp.sin(ang)
        q = jnp.dot(h, w_ref[:, :RET_QK], preferred_element_type=F32)
        _rope_store(q, c, s, 1.0, proj_ref, COL_Q)
        k = jnp.dot(h, w_ref[:, RET_QK:2 * RET_QK], preferred_element_type=F32)
        _rope_store(k, c, s, RET_QK_DIM ** -0.5, proj_ref, COL_K)

    @pl.when(jnp.logical_and(j >= 1, j < n_full))
    def _():
        proj_ref[...] = jnp.dot(h_sc[...], w_ref[...], preferred_element_type=F32).astype(BF16)

    if tail:
        @pl.when(j == n_full)
        def _():
            proj_ref[:, :tail] = jnp.dot(h_sc[...], w_ref[:, :tail],
                                         preferred_element_type=F32).astype(BF16)


def _inproj(x2, pos2, inv, g, w_main, w_dt, tm):
    m = x2.shape[0]
    tn = INPROJ_TN
    n_full, tail = divmod(PROJ_W, tn)
    assert COL_K + RET_QK == tn and tail % LANES == 0
    return pl.pallas_call(
        functools.partial(_inproj_kernel, n_full=n_full, tail=tail),
        grid=(m // tm, pl.cdiv(PROJ_W, tn)),
        in_specs=[
            pl.BlockSpec((tm, D_MODEL), lambda i, j: (i, 0)),
            pl.BlockSpec((tm, 1), lambda i, j: (i, 0)),
            pl.BlockSpec((1, ROPE_HALF), lambda i, j: (0, 0)),
            pl.BlockSpec((1, D_MODEL), lambda i, j: (0, 0)),
            pl.BlockSpec((D_MODEL, tn), lambda i, j: (0, j)),
            pl.BlockSpec((D_MODEL, SSM_GROUPS * LANES), lambda i, j: (0, 0)),
        ],
        out_specs=[
            pl.BlockSpec((tm, tn), lambda i, j: (i, j)),
            pl.BlockSpec((tm, SSM_GROUPS * LANES), lambda i, j: (i, 0)),
        ],
        out_shape=[
            jax.ShapeDtypeStruct((m, PROJ_W), BF16),
            jax.ShapeDtypeStruct((m, SSM_GROUPS * LANES), F32),
        ],
        scratch_shapes=[
            pltpu.VMEM((tm, D_MODEL), BF16),
        ],
        compiler_params=_cparams(("parallel", "arbitrary")),
        name="inproj",
    )(x2, pos2, inv, g, w_main, w_dt)


def _ret_kernel(logg_ref, q_ref, k_ref, v_ref, g_ref, gn_ref, o_ref, sb_sc, st_sc, y_sc, *, nc):
    lg = logg_ref[pl.program_id(1)]
    pos = lax.broadcasted_iota(jnp.int32, (RET_CHUNK, RET_QK_DIM), 0).astype(F32)
    qdec_f = jnp.exp((pos + 1.0) * lg)
    kdec_f = jnp.exp((RET_CHUNK - 1.0 - pos) * lg)
    qdec_b = jnp.exp((RET_CHUNK - pos) * lg)
    kdec_b = jnp.exp(pos * lg)
    chunk_dec = jnp.exp(jnp.full((1, RET_V_DIM), float(RET_CHUNK), F32) * lg)
    ii = lax.broadcasted_iota(jnp.int32, (RET_CHUNK, RET_CHUNK), 0)
    jj = lax.broadcasted_iota(jnp.int32, (RET_CHUNK, RET_CHUNK), 1)
    intra_dec = jnp.exp(jnp.abs(ii - jj).astype(F32) * lg)
    tdims = (((0,), (0,)), ((), ()))

    st_sc[...] = jnp.zeros_like(st_sc)

    def sweep_back(t, carry):
        c = nc - 1 - t
        rows = pl.ds(pl.multiple_of(c * RET_CHUNK, RET_CHUNK), RET_CHUNK)
        sb_sc[c] = st_sc[...].astype(BF16)
        kb = (k_ref[rows, :].astype(F32) * kdec_b).astype(BF16)
        upd = lax.dot_general(kb, v_ref[rows, :], tdims, preferred_element_type=F32)
        st_sc[...] = st_sc[...] * chunk_dec + upd
        return carry

    lax.fori_loop(0, nc, sweep_back, 0, unroll=2)

    st_sc[...] = jnp.zeros_like(st_sc)

    def chunk_rows(c):
        return pl.ds(pl.multiple_of(c * RET_CHUNK, RET_CHUNK), RET_CHUNK)

    def mix(c):
        rows = chunk_rows(c)
        qb16 = q_ref[rows, :]
        kb16 = k_ref[rows, :]
        vc = v_ref[rows, :]
        qf32 = qb16.astype(F32)
        scores = lax.dot_general(qb16, kb16, (((1,), (1,)), ((), ())), preferred_element_type=F32)
        y = jnp.dot((scores * intra_dec).astype(BF16), vc, preferred_element_type=F32)
        y = y + jnp.dot((qf32 * qdec_f).astype(BF16), st_sc[...].astype(BF16),
                        preferred_element_type=F32)
        y = y + jnp.dot((qf32 * qdec_b).astype(BF16), sb_sc[c], preferred_element_type=F32)
        y_sc[...] = y
        kf = (kb16.astype(F32) * kdec_f).astype(BF16)
        upd = lax.dot_general(kf, vc, tdims, preferred_element_type=F32)
        st_sc[...] = st_sc[...] * chunk_dec + upd

    def finish(c):
        rows = chunk_rows(c)
        y = y_sc[...]
        mu = jnp.mean(y, axis=-1, keepdims=True)
        d = y - mu
        var = jnp.mean(d * d, axis=-1, keepdims=True)
        yn = (d * lax.rsqrt(var + EPS)) * gn_ref[...]
        gate = g_ref[rows, :].astype(F32)
        o_ref[rows, :] = (yn * _silu(gate)).astype(BF16)

    mix(0)

    def sweep_fwd(c, carry):
        finish(c - 1)
        mix(c)
        return carry

    lax.fori_loop(1, nc, sweep_fwd, 0, unroll=2)
    finish(nc - 1)


def _retention(proj, logg, gn_g, bsz, seq):
    nc = seq // RET_CHUNK
    qb = COL_Q // RET_QK_DIM
    kb = COL_K // RET_QK_DIM
    vb = COL_V // RET_V_DIM
    gb = COL_G // RET_V_DIM
    return pl.pallas_call(
        functools.partial(_ret_kernel, nc=nc),
        grid=(bsz, RET_HEADS),
        in_specs=[
            pl.BlockSpec(memory_space=pltpu.SMEM),
            pl.BlockSpec((seq, RET_QK_DIM), lambda b, h: (b, qb + h)),
            pl.BlockSpec((seq, RET_QK_DIM), lambda b, h: (b, kb + h)),
            pl.BlockSpec((seq, RET_V_DIM), lambda b, h: (b, vb + h)),
            pl.BlockSpec((seq, RET_V_DIM), lambda b, h: (b, gb + h)),
            pl.BlockSpec((1, RET_V_DIM), lambda b, h: (0, h)),
        ],
        out_specs=pl.BlockSpec((seq, RET_V_DIM), lambda b, h: (b, h)),
        out_shape=jax.ShapeDtypeStruct((bsz * seq, RET_V), BF16),
        scratch_shapes=[
            pltpu.VMEM((nc, RET_QK_DIM, RET_V_DIM), BF16),
            pltpu.VMEM((RET_QK_DIM, RET_V_DIM), F32),
            pltpu.VMEM((RET_CHUNK, RET_V_DIM), F32),
        ],
        compiler_params=_cparams(("parallel", "arbitrary")),
        name="retention",
    )(logg, proj, proj, proj, proj, gn_g)


def _split_bf16(v):
    hi = v.astype(BF16)
    lo = (v - hi.astype(F32)).astype(BF16)
    return hi, lo


def _ssd_kernel(z_ref, x_ref, b_ref, c_ref, dt_ref, cwx_ref, cwb_ref, cwc_ref,
                cbx_ref, cbb_ref, cbc_ref, bias_ref, alog_ref, dskip_ref, ng_ref,
                o_ref, xs_sc, bm_sc, cm_sc, stb_sc, st_sc, sh_sc, e_sc,
                pcol_sc, prowt_sc, din_sc, wgt_sc, dtp_sc, y_sc, xw_sc, dch_sc, *, nc, pack):
    seq = nc * CHUNK
    hpg = SSM_HPG
    gw = SSM_GROUP_W
    cw = gw + 2 * SSM_STATE
    nblk = nc // pack
    tdims = (((0,), (0,)), ((), ()))

    def chunk_rows(c):
        return pl.ds(pl.multiple_of(c * CHUNK, CHUNK), CHUNK)

    si = lax.broadcasted_iota(jnp.int32, (CHUNK, CONV_K), 0)
    sr = lax.broadcasted_iota(jnp.int32, (CHUNK, CONV_K), 1)
    for idx, t in enumerate(CONV_SIDE_TAPS):
        sh_sc[idx] = jnp.where(sr == si + CONV_PAD + t - SSM_CONV // 2, 1.0, 0.0).astype(BF16)
    er = lax.broadcasted_iota(jnp.int32, (2 * LANES, 2 * gw), 0)
    ec = lax.broadcasted_iota(jnp.int32, (2 * LANES, 2 * gw), 1) // SSM_HEAD_DIM
    e_sc[...] = jnp.where(er - E_ORIGIN == ec, 1.0, 0.0).astype(BF16)

    def expand(v, slot, col0, ncol):
        e = e_sc[pl.ds(pl.multiple_of(E_ORIGIN - 2 * hpg * slot, 2 * hpg), LANES),
                 col0:col0 + ncol]
        return jnp.dot(v.astype(BF16), e, preferred_element_type=F32)

    lane = lax.broadcasted_iota(jnp.int32, (CHUNK, LANES), 1)
    fwd_lane = jnp.bitwise_and(lane, 2 * hpg - 1) < hpg
    first_slot = lane < 2 * hpg
    ii = lax.broadcasted_iota(jnp.int32, (CHUNK, CHUNK), 0)
    jj = lax.broadcasted_iota(jnp.int32, (CHUNK, CHUNK), 1)
    tril = jnp.where(jj <= ii, 1.0, 0.0).astype(F32)
    causal = jj <= ii
    diag = ii == jj
    pair_lo = lane < SSM_HEAD_DIM
    bias_row = bias_ref[0]
    a_row = -jnp.exp(alog_ref[0])

    def pre_body(blk, carry):
        raw = jnp.zeros((CHUNK, LANES), F32)
        for k in range(pack):
            t = jnp.where(first_slot, dt_ref[chunk_rows(blk * pack + k), :], 0.0)
            raw = raw + (pltpu.roll(t, 2 * hpg * k, axis=1) if k else t)
        dtp = jax.nn.softplus(raw + bias_row)
        la = dtp * a_row
        cum = jnp.dot(tril, la, preferred_element_type=F32, precision=lax.Precision.HIGHEST)
        tot = cum[CHUNK - 1:CHUNK, :]
        excl = cum - la
        pcol = jnp.where(fwd_lane, cum, tot - excl)
        pcol_sc[blk] = pcol * LOG2E
        prowt_sc[blk] = ((pcol - jnp.log(dtp)) * LOG2E).T
        din_sc[blk] = jnp.exp(pcol)
        wgt_sc[blk] = jnp.exp(jnp.where(fwd_lane, tot - cum, excl)) * dtp
        dtp_sc[blk] = dtp
        return carry

    lax.fori_loop(0, nblk, pre_body, 0)

    w_all = jnp.concatenate([cwx_ref[...], cwb_ref[...], cwc_ref[...]], axis=1)
    b_all = jnp.concatenate([cbx_ref[...], cbb_ref[...], cbc_ref[...]], axis=1)
    zero16 = jnp.zeros((), BF16)

    def conv_body(c, carry):
        r0 = pl.multiple_of(c * CHUNK, CHUNK)
        prev0 = pl.multiple_of(jnp.maximum(r0 - CONV_PAD, 0), CONV_PAD)
        next0 = pl.multiple_of(jnp.minimum(r0 + CHUNK, seq - CONV_PAD), CONV_PAD)
        pieces = []
        for src_ref in (x_ref, b_ref, c_ref):
            width = src_ref.shape[1]
            pieces.append(jnp.concatenate([
                jnp.where(c > 0, src_ref[pl.ds(prev0, CONV_PAD), :], zero16),
                src_ref[pl.ds(r0, CHUNK), :],
                jnp.where(c < nc - 1, src_ref[pl.ds(next0, CONV_PAD), :], zero16),
                jnp.zeros((CONV_K - CHUNK - 2 * CONV_PAD, width), BF16)], axis=0))
        stage = jnp.concatenate(pieces, axis=1)
        mid = SSM_CONV // 2
        acc = stage[CONV_PAD:CONV_PAD + CHUNK, :].astype(F32) * w_all[mid:mid + 1, :]
        for idx, t in enumerate(CONV_SIDE_TAPS):
            acc = acc + jnp.dot(sh_sc[idx], stage, preferred_element_type=F32) * w_all[t:t + 1, :]
        out = _silu(acc + b_all).astype(BF16)
        xs_sc[pl.ds(r0, CHUNK), :] = out[:, :gw]
        bm_sc[pl.ds(r0, CHUNK), :] = out[:, gw:gw + SSM_STATE]
        cm_sc[pl.ds(r0, CHUNK), :] = out[:, gw + SSM_STATE:]
        return carry

    lax.fori_loop(0, nc, conv_body, 0, unroll=4)

    def stage_back(c, slot_buf):
        blk = c // pack
        slot = c - blk * pack
        lhs = jnp.concatenate([wgt_sc[blk], din_sc[blk, 0:8, :]], axis=0)
        ex = expand(lhs, slot, gw, gw)
        xw_sc[slot_buf] = (xs_sc[chunk_rows(c), :].astype(F32) * ex[:CHUNK, :]).astype(BF16)
        dch_sc[slot_buf] = ex[CHUNK:CHUNK + 8, :]

    st_sc[...] = jnp.zeros_like(st_sc)
    stage_back(nc - 1, 0)

    def sweep_back(t, carry):
        c = nc - 1 - t
        buf = jnp.bitwise_and(t, 1)
        stb_sc[c] = st_sc[...].astype(BF16)
        upd = lax.dot_general(bm_sc[chunk_rows(c), :], xw_sc[buf], tdims,
                              preferred_element_type=F32)
        st_sc[...] = st_sc[...] * dch_sc[buf][0:1, :] + upd
        stage_back(jnp.maximum(c - 1, 0), 1 - buf)
        return carry

    lax.fori_loop(0, nc, sweep_back, 0, unroll=2)

    def mix(c):
        rows = chunk_rows(c)
        blk = c // pack
        slot = c - blk * pack
        unpack = jnp.bitwise_and(LANES - 2 * hpg * slot, LANES - 1)
        pcol = pltpu.roll(pcol_sc[blk], unpack, axis=1)
        prow_t = prowt_sc[blk, pl.ds(pl.multiple_of(2 * hpg * slot, 2 * hpg), 2 * hpg), :]
        xs = xs_sc[rows, :]
        bm = bm_sc[rows, :]
        cm = cm_sc[rows, :]
        cb = lax.dot_general(cm, bm, (((1,), (1,)), ((), ())), preferred_element_type=F32)
        cb_diag = jnp.sum(jnp.where(diag, cb, 0.0), axis=1, keepdims=True)
        ue = expand(jnp.where(fwd_lane, wgt_sc[blk], cb_diag * dtp_sc[blk]), slot, 0, 2 * gw)
        de = expand(din_sc[blk], slot, 0, 2 * gw)
        y_st = (jnp.dot(cm, st_sc[...].astype(BF16), preferred_element_type=F32) * de[:, :gw]
                + jnp.dot(cm, stb_sc[c], preferred_element_type=F32) * de[:, gw:])
        pieces = []
        for pr in range(hpg // 2):
            xs_p = xs[:, pr * LANES:(pr + 1) * LANES]
            m_pair = []
            for hh in (2 * pr, 2 * pr + 1):
                col_f = jnp.broadcast_to(pcol[:, hh:hh + 1], (CHUNK, CHUNK))
                col_b = jnp.broadcast_to(pcol[:, hpg + hh:hpg + hh + 1], (CHUNK, CHUNK))
                arg = jnp.where(causal, col_f - prow_t[hh:hh + 1, :],
                                col_b - prow_t[hpg + hh:hpg + hh + 1, :])
                m_pair.append((cb * jnp.exp2(arg)).astype(BF16))
            x_pair = jnp.concatenate([jnp.where(pair_lo, xs_p, zero16),
                                      jnp.where(pair_lo, zero16, xs_p)], axis=0)
            pieces.append(jnp.dot(jnp.concatenate(m_pair, axis=1), x_pair,
                                  preferred_element_type=F32))
        xs32 = xs.astype(F32)
        y_sc[...] = jnp.concatenate(pieces, axis=1) + y_st + (dskip_ref[...] + ue[:, gw:]) * xs32
        xw = (xs32 * ue[:, :gw]).astype(BF16)
        upd = lax.dot_general(bm, xw, tdims, preferred_element_type=F32)
        st_sc[...] = st_sc[...] * de[CHUNK - 1:CHUNK, :gw] + upd

    def finish(c):
        rows = chunk_rows(c)
        y = y_sc[...] * _silu(z_ref[rows, :].astype(F32))
        y = y * lax.rsqrt(jnp.mean(y * y, axis=-1, keepdims=True) + EPS)
        o_ref[rows, :] = (y * ng_ref[...]).astype(BF16)

    st_sc[...] = jnp.zeros_like(st_sc)
    mix(0)

    def sweep_fwd(c, carry):
        finish(c - 1)
        mix(c)
        return carry

    lax.fori_loop(1, nc, sweep_fwd, 0, unroll=2)
    finish(nc - 1)


def _ssd(proj, dt, cw_x, cw_b, cw_c, cb_x, cb_b, cb_c, bias_grp, alog_grp, dskip, norm_g, bsz, seq):
    nc = seq // CHUNK
    zb = COL_Z // SSM_GROUP_W
    xb = COL_X // SSM_GROUP_W
    bb = COL_B // SSM_STATE
    cb = COL_C // SSM_STATE
    gw = SSM_GROUP_W
    pack = min(LANES // PACK_LANES, nc)
    assert nc % pack == 0
    nblk = nc // pack
    packed = pltpu.VMEM((nblk, CHUNK, LANES), F32)
    return pl.pallas_call(
        functools.partial(_ssd_kernel, nc=nc, pack=pack),
        grid=(bsz, SSM_GROUPS),
        in_specs=[
            pl.BlockSpec((seq, gw), lambda b, g: (b, zb + g)),
            pl.BlockSpec((seq, gw), lambda b, g: (b, xb + g)),
            pl.BlockSpec((seq, SSM_STATE), lambda b, g: (b, bb + g)),
            pl.BlockSpec((seq, SSM_STATE), lambda b, g: (b, cb + g)),
            pl.BlockSpec((seq, LANES), lambda b, g: (b, g)),
            pl.BlockSpec((SSM_CONV, gw), lambda b, g: (0, g)),
            pl.BlockSpec((SSM_CONV, SSM_STATE), lambda b, g: (0, g)),
            pl.BlockSpec((SSM_CONV, SSM_STATE), lambda b, g: (0, g)),
            pl.BlockSpec((1, gw), lambda b, g: (0, g)),
            pl.BlockSpec((1, SSM_STATE), lambda b, g: (0, g)),
            pl.BlockSpec((1, SSM_STATE), lambda b, g: (0, g)),
            pl.BlockSpec((1, 1, LANES), lambda b, g: (g, 0, 0)),
            pl.BlockSpec((1, 1, LANES), lambda b, g: (g, 0, 0)),
            pl.BlockSpec((1, gw), lambda b, g: (0, g)),
            pl.BlockSpec((1, gw), lambda b, g: (0, g)),
        ],
        out_specs=pl.BlockSpec((seq, gw), lambda b, g: (b, g)),
        out_shape=jax.ShapeDtypeStruct((bsz * seq, SSM_INNER), BF16),
        scratch_shapes=[
            pltpu.VMEM((seq, gw), BF16),
            pltpu.VMEM((seq, SSM_STATE), BF16),
            pltpu.VMEM((seq, SSM_STATE), BF16),
            pltpu.VMEM((nc, SSM_STATE, gw), BF16),
            pltpu.VMEM((SSM_STATE, gw), F32),
            pltpu.VMEM((len(CONV_SIDE_TAPS), CHUNK, CONV_K), BF16),
            pltpu.VMEM((2 * LANES, 2 * gw), BF16),
            packed, packed, packed, packed, packed,
            pltpu.VMEM((CHUNK, gw), F32),
            pltpu.VMEM((2, CHUNK, gw), BF16),
            pltpu.VMEM((2, 8, gw), F32),
        ],
        compiler_params=_cparams(("parallel", "arbitrary")),
        name="ssd",
    )(proj, proj, proj, proj, dt, cw_x, cw_b, cw_c, cb_x, cb_b, cb_c, bias_grp, alog_grp,
      dskip, norm_g)


def _merge_kernel(yr_ref, ys_ref, gr_ref, gs_ref, x_ref, wr_ref, ws_ref, wo_ref, o_ref):
    y_ret = jnp.dot(yr_ref[...], wr_ref[...], preferred_element_type=F32)
    y_ssm = jnp.dot(ys_ref[...], ws_ref[...], preferred_element_type=F32)
    mixed = (_sigmoid(gr_ref[...].astype(F32)) * y_ret
             + _sigmoid(gs_ref[...].astype(F32)) * y_ssm)
    o_ref[...] = x_ref[...] + jnp.dot(mixed.astype(BF16), wo_ref[...], preferred_element_type=F32)


def _resident(shape):
    return pl.BlockSpec(shape, lambda i: (0,) * len(shape), pipeline_mode=pl.Buffered(1))


def _merge(yret, yssm, proj, x2, w_ret_o, w_ssm_o, w_out, tm):
    m = x2.shape[0]
    grb = COL_GATE_RET // D_MODEL
    gsb = COL_GATE_SSM // D_MODEL
    return pl.pallas_call(
        _merge_kernel,
        grid=(m // tm,),
        in_specs=[
            pl.BlockSpec((tm, RET_V), lambda i: (i, 0)),
            pl.BlockSpec((tm, SSM_INNER), lambda i: (i, 0)),
            pl.BlockSpec((tm, D_MODEL), lambda i: (i, grb)),
            pl.BlockSpec((tm, D_MODEL), lambda i: (i, gsb)),
            pl.BlockSpec((tm, D_MODEL), lambda i: (i, 0)),
            _resident((RET_V, D_MODEL)),
            _resident((SSM_INNER, D_MODEL)),
            _resident((D_MODEL, D_MODEL)),
        ],
        out_specs=pl.BlockSpec((tm, D_MODEL), lambda i: (i, 0)),
        out_shape=jax.ShapeDtypeStruct((m, D_MODEL), F32),
        compiler_params=_cparams(("parallel",)),
        name="merge",
    )(yret, yssm, proj, proj, x2, w_ret_o, w_ssm_o, w_out)


def _rms(x, g):
    return (x * lax.rsqrt(jnp.mean(x * x, axis=-1, keepdims=True) + EPS)) * g


def _mlp_kernel(x_ref, gm_ref, wu_ref, wd_ref, gf_ref, o_ref):
    x = x_ref[...]
    h = _rms(x, gm_ref[...]).astype(BF16)
    up = jnp.dot(h, wu_ref[...], preferred_element_type=F32)
    act = jnp.square(jnp.maximum(up, 0.0)).astype(BF16)
    x2 = x + jnp.dot(act, wd_ref[...], preferred_element_type=F32)
    o_ref[...] = _rms(x2, gf_ref[...])


def _mlp(x1, g_mlp, w_up, w_down, g_final, tm):
    m = x1.shape[0]
    return pl.pallas_call(
        _mlp_kernel,
        grid=(m // tm,),
        in_specs=[
            pl.BlockSpec((tm, D_MODEL), lambda i: (i, 0)),
            _resident((1, D_MODEL)),
            _resident((D_MODEL, D_FF)),
            _resident((D_FF, D_MODEL)),
            _resident((1, D_MODEL)),
        ],
        out_specs=pl.BlockSpec((tm, D_MODEL), lambda i: (i, 0)),
        out_shape=jax.ShapeDtypeStruct((m, D_MODEL), F32),
        compiler_params=_cparams(("parallel",)),
        name="mlp",
    )(x1, g_mlp, w_up, w_down, g_final)


def _group_rows(fwd, bwd):
    f = fwd.reshape(SSM_GROUPS, SSM_HPG)
    b = bwd.reshape(SSM_GROUPS, SSM_HPG)
    one = jnp.concatenate([f, b], axis=1)
    return jnp.tile(one, (1, LANES // PACK_LANES)).reshape(SSM_GROUPS, 1, LANES)


def _layer(x2, pos2, bsz, seq, norm_mix_g, w_in, ret_gn_g, w_ret_o, conv_w, conv_b, dt_bias_f,
           dt_bias_b, a_log_f, a_log_b, ssm_d, ssm_norm_g, w_ssm_o, w_out, norm_mlp_g, w_mlp_up,
           w_mlp_down, g_final):
    m = bsz * seq
    dt0 = COL_C + SSM_GROUPS * SSM_STATE
    w_main = jnp.concatenate([w_in[:, :dt0], w_in[:, dt0 + 2 * SSM_HEADS:]], axis=1).astype(BF16)
    wdt = w_in[:, dt0:dt0 + 2 * SSM_HEADS]
    wdt_f = wdt[:, :SSM_HEADS].reshape(D_MODEL, SSM_GROUPS, SSM_HPG)
    wdt_b = wdt[:, SSM_HEADS:].reshape(D_MODEL, SSM_GROUPS, SSM_HPG)
    wdt_pad = jnp.zeros((D_MODEL, SSM_GROUPS, LANES - 2 * SSM_HPG), F32)
    w_dt = jnp.concatenate([wdt_f, wdt_b, wdt_pad], axis=2).reshape(D_MODEL, SSM_GROUPS * LANES)
    w_dt = w_dt.astype(BF16)

    inv = (ROPE_BASE ** (-jnp.arange(ROPE_HALF, dtype=F32) / ROPE_HALF)).reshape(1, ROPE_HALF)
    tm1 = min(1024, m)
    proj, dt = _inproj(x2, pos2, inv, norm_mix_g.reshape(1, D_MODEL), w_main, w_dt, tm1)

    logg = jnp.log1p(-jnp.exp2(-5.0 - jnp.arange(RET_HEADS, dtype=F32)))
    yret = _retention(proj, logg, ret_gn_g.reshape(1, RET_V), bsz, seq)

    nbc = SSM_GROUPS * SSM_STATE
    cw_x, cw_b, cw_c = (conv_w[:, :SSM_INNER], conv_w[:, SSM_INNER:SSM_INNER + nbc],
                        conv_w[:, SSM_INNER + nbc:])
    cb2 = conv_b.reshape(1, -1)
    cb_x, cb_b, cb_c = (cb2[:, :SSM_INNER], cb2[:, SSM_INNER:SSM_INNER + nbc],
                        cb2[:, SSM_INNER + nbc:])
    dskip = jnp.repeat(ssm_d, SSM_HEAD_DIM).reshape(1, SSM_INNER)
    yssm = _ssd(proj, dt, cw_x, cw_b, cw_c, cb_x, cb_b, cb_c,
                _group_rows(dt_bias_f, dt_bias_b), _group_rows(a_log_f, a_log_b),
                dskip, ssm_norm_g.reshape(1, SSM_INNER), bsz, seq)

    tm4 = min(512, m)
    x1 = _merge(yret, yssm, proj, x2, w_ret_o.astype(BF16), w_ssm_o.astype(BF16),
                w_out.astype(BF16), tm4)
    return _mlp(x1, norm_mlp_g.reshape(1, D_MODEL), w_mlp_up.astype(BF16),
                w_mlp_down.astype(BF16), g_final, tm4)


def kernel(x, positions, norm_mix_g, w_in, ret_gn_g, w_ret_o, conv_w, conv_b, dt_bias_f, dt_bias_b,
           a_log_f, a_log_b, ssm_d, ssm_norm_g, w_ssm_o, w_out, norm_mlp_g, w_mlp_up, w_mlp_down,
           norm_final_g):
    bsz, seq, _ = x.shape
    depth = w_in.shape[0]
    assert depth == 1, "the final-norm fusion in the MLP call assumes a single layer"
    x2 = x.reshape(bsz * seq, D_MODEL)
    pos2 = positions.reshape(bsz * seq, 1)
    out = _layer(x2, pos2, bsz, seq, norm_mix_g[0], w_in[0], ret_gn_g[0], w_ret_o[0], conv_w[0],
                 conv_b[0], dt_bias_f[0], dt_bias_b[0], a_log_f[0], a_log_b[0], ssm_d[0],
                 ssm_norm_g[0], w_ssm_o[0], w_out[0], norm_mlp_g[0], w_mlp_up[0], w_mlp_down[0],
                 norm_final_g.reshape(1, D_MODEL))
    return out.reshape(bsz, seq, D_MODEL)
```

```python
import functools

import jax
import jax.numpy as jnp
from jax import lax
from jax.experimental import pallas as pl
from jax.experimental.pallas import tpu as pltpu

F32 = jnp.float32
BF16 = jnp.bfloat16

D_MODEL = 1024
RET_HEADS = 4
RET_QK_DIM = 256
RET_V_DIM = 512
RET_QK = RET_HEADS * RET_QK_DIM
RET_V = RET_HEADS * RET_V_DIM
ROPE_BASE = 10000.0
ROPE_HALF = RET_QK_DIM // 2
SSM_INNER = 2 * D_MODEL
SSM_HEAD_DIM = 64
SSM_HEADS = SSM_INNER // SSM_HEAD_DIM
SSM_GROUPS = 4
SSM_HPG = SSM_HEADS // SSM_GROUPS
SSM_STATE = 128
SSM_CONV = 5
SSM_GROUP_W = SSM_HPG * SSM_HEAD_DIM
D_FF = 4 * D_MODEL
EPS = 1e-6
CHUNK = 128
RET_CHUNK = 256

LANES = 128
BF16_ROWS = 16
MXU_K = 256
CONV_PAD = BF16_ROWS
CONV_K = MXU_K
CONV_SIDE_TAPS = tuple(t for t in range(SSM_CONV) if t != SSM_CONV // 2)
PACK_LANES = 2 * SSM_HPG
E_ORIGIN = LANES - PACK_LANES
LOG2E = 1.4426950408889634

COL_Q = 0
COL_K = COL_Q + RET_QK
COL_V = COL_K + RET_QK
COL_G = COL_V + RET_V
COL_Z = COL_G + RET_V
COL_X = COL_Z + SSM_INNER
COL_B = COL_X + SSM_INNER
COL_C = COL_B + SSM_GROUPS * SSM_STATE
COL_GATE_RET = COL_C + SSM_GROUPS * SSM_STATE
COL_GATE_SSM = COL_GATE_RET + D_MODEL
PROJ_W = COL_GATE_SSM + D_MODEL
INPROJ_TN = 2 * RET_QK

VMEM_LIMIT = 56 * 1024 * 1024


def _cparams(sem):
    return pltpu.CompilerParams(dimension_semantics=sem, vmem_limit_bytes=VMEM_LIMIT)


def _sigmoid(x):
    return 0.5 * jnp.tanh(0.5 * x) + 0.5


def _silu(x):
    return x * _sigmoid(x)


def _rope_store(acc, c, s, scale, proj_ref, col0):
    for hd in range(RET_HEADS):
        lo = hd * RET_QK_DIM
        t1 = acc[:, lo:lo + ROPE_HALF]
        t2 = acc[:, lo + ROPE_HALF:lo + RET_QK_DIM]
        o = col0 + lo
        proj_ref[:, o:o + ROPE_HALF] = ((t1 * c - t2 * s) * scale).astype(BF16)
        proj_ref[:, o + ROPE_HALF:o + RET_QK_DIM] = ((t1 * s + t2 * c) * scale).astype(BF16)


TAIL_STEP = 1


def _inproj_col_tile(j, n_full):
    return jnp.where(j == 0, 0, jnp.where(j == TAIL_STEP, n_full, j - 1))


def _inproj_kernel(x_ref, pos_ref, inv_ref, g_ref, w_ref, wdt_ref, proj_ref, dt_ref, h_sc, *, tail):
    j = pl.program_id(1)

    @pl.when(j == 0)
    def _():
        x = x_ref[...]
        ms = jnp.mean(x * x, axis=-1, keepdims=True)
        h = ((x * lax.rsqrt(ms + EPS)) * g_ref[...]).astype(BF16)
        h_sc[...] = h
        dt_ref[...] = jnp.dot(h, wdt_ref[...], preferred_element_type=F32)
        ang = pos_ref[...] * inv_ref[...]
        c = jnp.cos(ang)
        s = jnp.sin(ang)
        q = jnp.dot(h, w_ref[:, :RET_QK], preferred_element_type=F32)
        _rope_store(q, c, s, 1.0, proj_ref, COL_Q)
        k = jnp.dot(h, w_ref[:, RET_QK:2 * RET_QK], preferred_element_type=F32)
        _rope_store(k, c, s, RET_QK_DIM ** -0.5, proj_ref, COL_K)

    @pl.when(j > TAIL_STEP)
    def _():
        proj_ref[...] = jnp.dot(h_sc[...], w_ref[...], preferred_element_type=F32).astype(BF16)

    @pl.when(j == TAIL_STEP)
    def _():
        proj_ref[:, :tail] = jnp.dot(h_sc[...], w_ref[:, :tail],
                                     preferred_element_type=F32).astype(BF16)


def _inproj(x2, pos2, inv, g, w_main, w_dt, tm):
    m = x2.shape[0]
    tn = INPROJ_TN
    n_full, tail = divmod(PROJ_W, tn)
    assert COL_K + RET_QK == tn and tail > 0 and tail % LANES == 0 and n_full > TAIL_STEP
    col = functools.partial(_inproj_col_tile, n_full=n_full)
    return pl.pallas_call(
        functools.partial(_inproj_kernel, tail=tail),
        grid=(m // tm, n_full + 1),
        in_specs=[
            pl.BlockSpec((tm, D_MODEL), lambda i, j: (i, 0)),
            pl.BlockSpec((tm, ROPE_HALF), lambda i, j: (i, 0)),
            pl.BlockSpec((1, ROPE_HALF), lambda i, j: (0, 0)),
            pl.BlockSpec((1, D_MODEL), lambda i, j: (0, 0)),
            pl.BlockSpec((D_MODEL, tn), lambda i, j: (0, col(j))),
            pl.BlockSpec((D_MODEL, SSM_GROUPS * LANES), lambda i, j: (0, 0)),
        ],
        out_specs=[
            pl.BlockSpec((tm, tn), lambda i, j: (i, col(j))),
            pl.BlockSpec((tm, SSM_GROUPS * LANES), lambda i, j: (i, 0)),
        ],
        out_shape=[
            jax.ShapeDtypeStruct((m, PROJ_W), BF16),
            jax.ShapeDtypeStruct((m, SSM_GROUPS * LANES), F32),
        ],
        scratch_shapes=[
            pltpu.VMEM((tm, D_MODEL), BF16),
        ],
        compiler_params=_cparams(("parallel", "arbitrary")),
        name="inproj",
    )(x2, pos2, inv, g, w_main, w_dt)


def _ret_kernel(logg_ref, q_ref, k_ref, v_ref, g_ref, gn_ref, o_ref, sb_sc, st_sc, y_sc, *, nc):
    lg = logg_ref[pl.program_id(1)]
    pos = lax.broadcasted_iota(jnp.int32, (RET_CHUNK, RET_QK_DIM), 0).astype(F32)
    qdec_f = jnp.exp((pos + 1.0) * lg)
    kdec_f = jnp.exp((RET_CHUNK - 1.0 - pos) * lg)
    qdec_b = jnp.exp((RET_CHUNK - pos) * lg)
    kdec_b = jnp.exp(pos * lg)
    chunk_dec = jnp.exp(jnp.full((1, RET_V_DIM), float(RET_CHUNK), F32) * lg)
    ii = lax.broadcasted_iota(jnp.int32, (RET_CHUNK, RET_CHUNK), 0)
    jj = lax.broadcasted_iota(jnp.int32, (RET_CHUNK, RET_CHUNK), 1)
    intra_dec = jnp.exp(jnp.abs(ii - jj).astype(F32) * lg)
    tdims = (((0,), (0,)), ((), ()))

    st_sc[...] = jnp.zeros_like(st_sc)

    def sweep_back(t, carry):
        c = nc - 1 - t
        rows = pl.ds(pl.multiple_of(c * RET_CHUNK, RET_CHUNK), RET_CHUNK)
        sb_sc[c] = st_sc[...].astype(BF16)
        kb = (k_ref[rows, :].astype(F32) * kdec_b).astype(BF16)
        upd = lax.dot_general(kb, v_ref[rows, :], tdims, preferred_element_type=F32)
        st_sc[...] = st_sc[...] * chunk_dec + upd
        return carry

    lax.fori_loop(0, nc, sweep_back, 0, unroll=2)

    st_sc[...] = jnp.zeros_like(st_sc)

    def chunk_rows(c):
        return pl.ds(pl.multiple_of(c * RET_CHUNK, RET_CHUNK), RET_CHUNK)

    def mix(c):
        rows = chunk_rows(c)
        qb16 = q_ref[rows, :]
        kb16 = k_ref[rows, :]
        vc = v_ref[rows, :]
        qf32 = qb16.astype(F32)
        scores = lax.dot_general(qb16, kb16, (((1,), (1,)), ((), ())), preferred_element_type=F32)
        y = jnp.dot((scores * intra_dec).astype(BF16), vc, preferred_element_type=F32)
        y = y + jnp.dot((qf32 * qdec_f).astype(BF16), st_sc[...].astype(BF16),
                        preferred_element_type=F32)
        y = y + jnp.dot((qf32 * qdec_b).astype(BF16), sb_sc[c], preferred_element_type=F32)
        y_sc[...] = y
        kf = (kb16.astype(F32) * kdec_f).astype(BF16)
        upd = lax.dot_general(kf, vc, tdims, preferred_element_type=F32)
        st_sc[...] = st_sc[...] * chunk_dec + upd

    def finish(c):
        rows = chunk_rows(c)
        y = y_sc[...]
        mu = jnp.mean(y, axis=-1, keepdims=True)
        d = y - mu
        var = jnp.mean(d * d, axis=-1, keepdims=True)
        yn = (d * lax.rsqrt(var + EPS)) * gn_ref[...]
        gate = g_ref[rows, :].astype(F32)
        o_ref[rows, :] = (yn * _silu(gate)).astype(BF16)

    mix(0)

    def sweep_fwd(c, carry):
        finish(c - 1)
        mix(c)
        return carry

    lax.fori_loop(1, nc, sweep_fwd, 0, unroll=2)
    finish(nc - 1)


def _retention(proj, logg, gn_g, bsz, seq):
    nc = seq // RET_CHUNK
    qb = COL_Q // RET_QK_DIM
    kb = COL_K // RET_QK_DIM
    vb = COL_V // RET_V_DIM
    gb = COL_G // RET_V_DIM
    return pl.pallas_call(
        functools.partial(_ret_kernel, nc=nc),
        grid=(bsz, RET_HEADS),
        in_specs=[
            pl.BlockSpec(memory_space=pltpu.SMEM),
            pl.BlockSpec((seq, RET_QK_DIM), lambda b, h: (b, qb + h)),
            pl.BlockSpec((seq, RET_QK_DIM), lambda b, h: (b, kb + h)),
            pl.BlockSpec((seq, RET_V_DIM), lambda b, h: (b, vb + h)),
            pl.BlockSpec((seq, RET_V_DIM), lambda b, h: (b, gb + h)),
            pl.BlockSpec((1, RET_V_DIM), lambda b, h: (0, h)),
        ],
        out_specs=pl.BlockSpec((seq, RET_V_DIM), lambda b, h: (b, h)),
        out_shape=jax.ShapeDtypeStruct((bsz * seq, RET_V), BF16),
        scratch_shapes=[
            pltpu.VMEM((nc, RET_QK_DIM, RET_V_DIM), BF16),
            pltpu.VMEM((RET_QK_DIM, RET_V_DIM), F32),
            pltpu.VMEM((RET_CHUNK, RET_V_DIM), F32),
        ],
        compiler_params=_cparams(("parallel", "arbitrary")),
        name="retention",
    )(logg, proj, proj, proj, proj, gn_g)


def _split_bf16(v):
    hi = v.astype(BF16)
    lo = (v - hi.astype(F32)).astype(BF16)
    return hi, lo


def _ssd_kernel(z_ref, x_ref, b_ref, c_ref, dt_ref, cwx_ref, cwb_ref, cwc_ref,
                cbx_ref, cbb_ref, cbc_ref, bias_ref, alog_ref, dskip_ref, ng_ref,
                o_ref, xs_sc, bm_sc, cm_sc, stb_sc, st_sc, sh_sc, e_sc,
                pcol_sc, prowt_sc, din_sc, wgt_sc, dtp_sc, y_sc, xw_sc, dch_sc, *, nc, pack):
    seq = nc * CHUNK
    hpg = SSM_HPG
    gw = SSM_GROUP_W
    cw = gw + 2 * SSM_STATE
    nblk = nc // pack
    tdims = (((0,), (0,)), ((), ()))

    def chunk_rows(c):
        return pl.ds(pl.multiple_of(c * CHUNK, CHUNK), CHUNK)

    si = lax.broadcasted_iota(jnp.int32, (CHUNK, CONV_K), 0)
    sr = lax.broadcasted_iota(jnp.int32, (CHUNK, CONV_K), 1)
    for idx, t in enumerate(CONV_SIDE_TAPS):
        sh_sc[idx] = jnp.where(sr == si + CONV_PAD + t - SSM_CONV // 2, 1.0, 0.0).astype(BF16)
    er = lax.broadcasted_iota(jnp.int32, (2 * LANES, 2 * gw), 0)
    ec = lax.broadcasted_iota(jnp.int32, (2 * LANES, 2 * gw), 1) // SSM_HEAD_DIM
    e_sc[...] = jnp.where(er - E_ORIGIN == ec, 1.0, 0.0).astype(BF16)

    def expand(v, slot, col0, ncol):
        e = e_sc[pl.ds(pl.multiple_of(E_ORIGIN - 2 * hpg * slot, 2 * hpg), LANES),
                 col0:col0 + ncol]
        return jnp.dot(v.astype(BF16), e, preferred_element_type=F32)

    lane = lax.broadcasted_iota(jnp.int32, (CHUNK, LANES), 1)
    fwd_lane = jnp.bitwise_and(lane, 2 * hpg - 1) < hpg
    first_slot = lane < 2 * hpg
    ii = lax.broadcasted_iota(jnp.int32, (CHUNK, CHUNK), 0)
    jj = lax.broadcasted_iota(jnp.int32, (CHUNK, CHUNK), 1)
    tril = jnp.where(jj <= ii, 1.0, 0.0).astype(F32)
    causal = jj <= ii
    diag = ii == jj
    pair_lo = lane < SSM_HEAD_DIM
    bias_row = bias_ref[0]
    a_row = -jnp.exp(alog_ref[0])

    def pre_body(blk, carry):
        raw = jnp.zeros((CHUNK, LANES), F32)
        for k in range(pack):
            t = jnp.where(first_slot, dt_ref[chunk_rows(blk * pack + k), :], 0.0)
            raw = raw + (pltpu.roll(t, 2 * hpg * k, axis=1) if k else t)
        dtp = jax.nn.softplus(raw + bias_row)
        la = dtp * a_row
        cum = jnp.dot(tril, la, preferred_element_type=F32, precision=lax.Precision.HIGHEST)
        tot = cum[CHUNK - 1:CHUNK, :]
        excl = cum - la
        pcol = jnp.where(fwd_lane, cum, tot - excl)
        pcol_sc[blk] = pcol * LOG2E
        prowt_sc[blk] = ((pcol - jnp.log(dtp)) * LOG2E).T
        din_sc[blk] = jnp.exp(pcol)
        wgt_sc[blk] = jnp.exp(jnp.where(fwd_lane, tot - cum, excl)) * dtp
        dtp_sc[blk] = dtp
        return carry

    lax.fori_loop(0, nblk, pre_body, 0)

    w_all = jnp.concatenate([cwx_ref[...], cwb_ref[...], cwc_ref[...]], axis=1)
    b_all = jnp.concatenate([cbx_ref[...], cbb_ref[...], cbc_ref[...]], axis=1)
    zero16 = jnp.zeros((), BF16)

    def conv_body(c, carry):
        r0 = pl.multiple_of(c * CHUNK, CHUNK)
        prev0 = pl.multiple_of(jnp.maximum(r0 - CONV_PAD, 0), CONV_PAD)
        next0 = pl.multiple_of(jnp.minimum(r0 + CHUNK, seq - CONV_PAD), CONV_PAD)
        pieces = []
        for src_ref in (x_ref, b_ref, c_ref):
            width = src_ref.shape[1]
            pieces.append(jnp.concatenate([
                jnp.where(c > 0, src_ref[pl.ds(prev0, CONV_PAD), :], zero16),
                src_ref[pl.ds(r0, CHUNK), :],
                jnp.where(c < nc - 1, src_ref[pl.ds(next0, CONV_PAD), :], zero16),
                jnp.zeros((CONV_K - CHUNK - 2 * CONV_PAD, width), BF16)], axis=0))
        stage = jnp.concatenate(pieces, axis=1)
        mid = SSM_CONV // 2
        acc = stage[CONV_PAD:CONV_PAD + CHUNK, :].astype(F32) * w_all[mid:mid + 1, :]
        for idx, t in enumerate(CONV_SIDE_TAPS):
            acc = acc + jnp.dot(sh_sc[idx], stage, preferred_element_type=F32) * w_all[t:t + 1, :]
        out = _silu(acc + b_all).astype(BF16)
        xs_sc[pl.ds(r0, CHUNK), :] = out[:, :gw]
        bm_sc[pl.ds(r0, CHUNK), :] = out[:, gw:gw + SSM_STATE]
        cm_sc[pl.ds(r0, CHUNK), :] = out[:, gw + SSM_STATE:]
        return carry

    lax.fori_loop(0, nc, conv_body, 0, unroll=4)

    def stage_back(c, slot_buf):
        blk = c // pack
        slot = c - blk * pack
        lhs = jnp.concatenate([wgt_sc[blk], din_sc[blk, 0:8, :]], axis=0)
        ex = expand(lhs, slot, gw, gw)
        xw_sc[slot_buf] = (xs_sc[chunk_rows(c), :].astype(F32) * ex[:CHUNK, :]).astype(BF16)
        dch_sc[slot_buf] = ex[CHUNK:CHUNK + 8, :]

    st_sc[...] = jnp.zeros_like(st_sc)
    stage_back(nc - 1, 0)

    def sweep_back(t, carry):
        c = nc - 1 - t
        buf = jnp.bitwise_and(t, 1)
        stb_sc[c] = st_sc[...].astype(BF16)
        upd = lax.dot_general(bm_sc[chunk_rows(c), :], xw_sc[buf], tdims,
                              preferred_element_type=F32)
        st_sc[...] = st_sc[...] * dch_sc[buf][0:1, :] + upd
        stage_back(jnp.maximum(c - 1, 0), 1 - buf)
        return carry

    lax.fori_loop(0, nc, sweep_back, 0, unroll=2)

    def mix(c):
        rows = chunk_rows(c)
        blk = c // pack
        slot = c - blk * pack
        unpack = jnp.bitwise_and(LANES - 2 * hpg * slot, LANES - 1)
        pcol = pltpu.roll(pcol_sc[blk], unpack, axis=1)
        prow_t = prowt_sc[blk, pl.ds(pl.multiple_of(2 * hpg * slot, 2 * hpg), 2 * hpg), :]
        xs = xs_sc[rows, :]
        bm = bm_sc[rows, :]
        cm = cm_sc[rows, :]
        cb = lax.dot_general(cm, bm, (((1,), (1,)), ((), ())), preferred_element_type=F32)
        cb_diag = jnp.sum(jnp.where(diag, cb, 0.0), axis=1, keepdims=True)
        ue = expand(jnp.where(fwd_lane, wgt_sc[blk], cb_diag * dtp_sc[blk]), slot, 0, 2 * gw)
        de = expand(din_sc[blk], slot, 0, 2 * gw)
        y_st = (jnp.dot(cm, st_sc[...].astype(BF16), preferred_element_type=F32) * de[:, :gw]
                + jnp.dot(cm, stb_sc[c], preferred_element_type=F32) * de[:, gw:])
        pieces = []
        for pr in range(hpg // 2):
            xs_p = xs[:, pr * LANES:(pr + 1) * LANES]
            m_pair = []
            for hh in (2 * pr, 2 * pr + 1):
                col_f = jnp.broadcast_to(pcol[:, hh:hh + 1], (CHUNK, CHUNK))
                col_b = jnp.broadcast_to(pcol[:, hpg + hh:hpg + hh + 1], (CHUNK, CHUNK))
                arg = jnp.where(causal, col_f - prow_t[hh:hh + 1, :],
                                col_b - prow_t[hpg + hh:hpg + hh + 1, :])
                m_pair.append((cb * jnp.exp2(arg)).astype(BF16))
            x_pair = jnp.concatenate([jnp.where(pair_lo, xs_p, zero16),
                                      jnp.where(pair_lo, zero16, xs_p)], axis=0)
            pieces.append(jnp.dot(jnp.concatenate(m_pair, axis=1), x_pair,
                                  preferred_element_type=F32))
        xs32 = xs.astype(F32)
        y_sc[...] = jnp.concatenate(pieces, axis=1) + y_st + (dskip_ref[...] + ue[:, gw:]) * xs32
        xw = (xs32 * ue[:, :gw]).astype(BF16)
        upd = lax.dot_general(bm, xw, tdims, preferred_element_type=F32)
        st_sc[...] = st_sc[...] * de[CHUNK - 1:CHUNK, :gw] + upd

    def finish(c):
        rows = chunk_rows(c)
        y = y_sc[...] * _silu(z_ref[rows, :].astype(F32))
        y = y * lax.rsqrt(jnp.mean(y * y, axis=-1, keepdims=True) + EPS)
        o_ref[rows, :] = (y * ng_ref[...]).astype(BF16)

    st_sc[...] = jnp.zeros_like(st_sc)
    mix(0)

    def sweep_fwd(c, carry):
        finish(c - 1)
        mix(c)
        return carry

    lax.fori_loop(1, nc, sweep_fwd, 0, unroll=2)
    finish(nc - 1)


def _ssd(proj, dt, cw_x, cw_b, cw_c, cb_x, cb_b, cb_c, bias_grp, alog_grp, dskip, norm_g, bsz, seq):
    nc = seq // CHUNK
    zb = COL_Z // SSM_GROUP_W
    xb = COL_X // SSM_GROUP_W
    bb = COL_B // SSM_STATE
    cb = COL_C // SSM_STATE
    gw = SSM_GROUP_W
    pack = min(LANES // PACK_LANES, nc)
    assert nc % pack == 0
    nblk = nc // pack
    packed = pltpu.VMEM((nblk, CHUNK, LANES), F32)
    return pl.pallas_call(
        functools.partial(_ssd_kernel, nc=nc, pack=pack),
        grid=(bsz, SSM_GROUPS),
        in_specs=[
            pl.BlockSpec((seq, gw), lambda b, g: (b, zb + g)),
            pl.BlockSpec((seq, gw), lambda b, g: (b, xb + g)),
            pl.BlockSpec((seq, SSM_STATE), lambda b, g: (b, bb + g)),
            pl.BlockSpec((seq, SSM_STATE), lambda b, g: (b, cb + g)),
            pl.BlockSpec((seq, LANES), lambda b, g: (b, g)),
            pl.BlockSpec((SSM_CONV, gw), lambda b, g: (0, g)),
            pl.BlockSpec((SSM_CONV, SSM_STATE), lambda b, g: (0, g)),
            pl.BlockSpec((SSM_CONV, SSM_STATE), lambda b, g: (0, g)),
            pl.BlockSpec((1, gw), lambda b, g: (0, g)),
            pl.BlockSpec((1, SSM_STATE), lambda b, g: (0, g)),
            pl.BlockSpec((1, SSM_STATE), lambda b, g: (0, g)),
            pl.BlockSpec((1, 1, LANES), lambda b, g: (g, 0, 0)),
            pl.BlockSpec((1, 1, LANES), lambda b, g: (g, 0, 0)),
            pl.BlockSpec((1, gw), lambda b, g: (0, g)),
            pl.BlockSpec((1, gw), lambda b, g: (0, g)),
        ],
        out_specs=pl.BlockSpec((seq, gw), lambda b, g: (b, g)),
        out_shape=jax.ShapeDtypeStruct((bsz * seq, SSM_INNER), BF16),
        scratch_shapes=[
            pltpu.VMEM((seq, gw), BF16),
            pltpu.VMEM((seq, SSM_STATE), BF16),
            pltpu.VMEM((seq, SSM_STATE), BF16),
            pltpu.VMEM((nc, SSM_STATE, gw), BF16),
            pltpu.VMEM((SSM_STATE, gw), F32),
            pltpu.VMEM((len(CONV_SIDE_TAPS), CHUNK, CONV_K), BF16),
            pltpu.VMEM((2 * LANES, 2 * gw), BF16),
            packed, packed, packed, packed, packed,
            pltpu.VMEM((CHUNK, gw), F32),
            pltpu.VMEM((2, CHUNK, gw), BF16),
            pltpu.VMEM((2, 8, gw), F32),
        ],
        compiler_params=_cparams(("parallel", "arbitrary")),
        name="ssd",
    )(proj, proj, proj, proj, dt, cw_x, cw_b, cw_c, cb_x, cb_b, cb_c, bias_grp, alog_grp,
      dskip, norm_g)


def _merge_kernel(yr_ref, ys_ref, gr_ref, gs_ref, x_ref, wr_ref, ws_ref, wo_ref, o_ref):
    y_ret = jnp.dot(yr_ref[...], wr_ref[...], preferred_element_type=F32)
    y_ssm = jnp.dot(ys_ref[...], ws_ref[...], preferred_element_type=F32)
    mixed = (_sigmoid(gr_ref[...].astype(F32)) * y_ret
             + _sigmoid(gs_ref[...].astype(F32)) * y_ssm)
    o_ref[...] = x_ref[...] + jnp.dot(mixed.astype(BF16), wo_ref[...], preferred_element_type=F32)


def _resident(shape):
    return pl.BlockSpec(shape, lambda i: (0,) * len(shape), pipeline_mode=pl.Buffered(1))


def _merge(yret, yssm, proj, x2, w_ret_o, w_ssm_o, w_out, tm):
    m = x2.shape[0]
    grb = COL_GATE_RET // D_MODEL
    gsb = COL_GATE_SSM // D_MODEL
    return pl.pallas_call(
        _merge_kernel,
        grid=(m // tm,),
        in_specs=[
            pl.BlockSpec((tm, RET_V), lambda i: (i, 0)),
            pl.BlockSpec((tm, SSM_INNER), lambda i: (i, 0)),
            pl.BlockSpec((tm, D_MODEL), lambda i: (i, grb)),
            pl.BlockSpec((tm, D_MODEL), lambda i: (i, gsb)),
            pl.BlockSpec((tm, D_MODEL), lambda i: (i, 0)),
            _resident((RET_V, D_MODEL)),
            _resident((SSM_INNER, D_MODEL)),
            _resident((D_MODEL, D_MODEL)),
        ],
        out_specs=pl.BlockSpec((tm, D_MODEL), lambda i: (i, 0)),
        out_shape=jax.ShapeDtypeStruct((m, D_MODEL), F32),
        compiler_params=_cparams(("parallel",)),
        name="merge",
    )(yret, yssm, proj, proj, x2, w_ret_o, w_ssm_o, w_out)


def _rms(x, g):
    return (x * lax.rsqrt(jnp.mean(x * x, axis=-1, keepdims=True) + EPS)) * g


def _mlp_kernel(x_ref, gm_ref, wu_ref, wd_ref, gf_ref, o_ref):
    x = x_ref[...]
    h = _rms(x, gm_ref[...]).astype(BF16)
    up = jnp.dot(h, wu_ref[...], preferred_element_type=F32)
    act = jnp.square(jnp.maximum(up, 0.0)).astype(BF16)
    x2 = x + jnp.dot(act, wd_ref[...], preferred_element_type=F32)
    o_ref[...] = _rms(x2, gf_ref[...])


def _mlp(x1, g_mlp, w_up, w_down, g_final, tm):
    m = x1.shape[0]
    return pl.pallas_call(
        _mlp_kernel,
        grid=(m // tm,),
        in_specs=[
            pl.BlockSpec((tm, D_MODEL), lambda i: (i, 0)),
            _resident((1, D_MODEL)),
            _resident((D_MODEL, D_FF)),
            _resident((D_FF, D_MODEL)),
            _resident((1, D_MODEL)),
        ],
        out_specs=pl.BlockSpec((tm, D_MODEL), lambda i: (i, 0)),
        out_shape=jax.ShapeDtypeStruct((m, D_MODEL), F32),
        compiler_params=_cparams(("parallel",)),
        name="mlp",
    )(x1, g_mlp, w_up, w_down, g_final)


def _group_rows(fwd, bwd):
    f = fwd.reshape(SSM_GROUPS, SSM_HPG)
    b = bwd.reshape(SSM_GROUPS, SSM_HPG)
    one = jnp.concatenate([f, b], axis=1)
    return jnp.tile(one, (1, LANES // PACK_LANES)).reshape(SSM_GROUPS, 1, LANES)


def _layer(x2, pos2, bsz, seq, norm_mix_g, w_in, ret_gn_g, w_ret_o, conv_w, conv_b, dt_bias_f,
           dt_bias_b, a_log_f, a_log_b, ssm_d, ssm_norm_g, w_ssm_o, w_out, norm_mlp_g, w_mlp_up,
           w_mlp_down, g_final):
    m = bsz * seq
    dt0 = COL_C + SSM_GROUPS * SSM_STATE
    w_main = jnp.concatenate([w_in[:, :dt0], w_in[:, dt0 + 2 * SSM_HEADS:]], axis=1).astype(BF16)
    wdt = w_in[:, dt0:dt0 + 2 * SSM_HEADS]
    wdt_f = wdt[:, :SSM_HEADS].reshape(D_MODEL, SSM_GROUPS, SSM_HPG)
    wdt_b = wdt[:, SSM_HEADS:].reshape(D_MODEL, SSM_GROUPS, SSM_HPG)
    wdt_pad = jnp.zeros((D_MODEL, SSM_GROUPS, LANES - 2 * SSM_HPG), F32)
    w_dt = jnp.concatenate([wdt_f, wdt_b, wdt_pad], axis=2).reshape(D_MODEL, SSM_GROUPS * LANES)
    w_dt = w_dt.astype(BF16)

    inv = (ROPE_BASE ** (-jnp.arange(ROPE_HALF, dtype=F32) / ROPE_HALF)).reshape(1, ROPE_HALF)
    tm1 = min(1024, m)
    proj, dt = _inproj(x2, pos2, inv, norm_mix_g.reshape(1, D_MODEL), w_main, w_dt, tm1)

    logg = jnp.log1p(-jnp.exp2(-5.0 - jnp.arange(RET_HEADS, dtype=F32)))
    yret = _retention(proj, logg, ret_gn_g.reshape(1, RET_V), bsz, seq)

    nbc = SSM_GROUPS * SSM_STATE
    cw_x, cw_b, cw_c = (conv_w[:, :SSM_INNER], conv_w[:, SSM_INNER:SSM_INNER + nbc],
                        conv_w[:, SSM_INNER + nbc:])
    cb2 = conv_b.reshape(1, -1)
    cb_x, cb_b, cb_c = (cb2[:, :SSM_INNER], cb2[:, SSM_INNER:SSM_INNER + nbc],
                        cb2[:, SSM_INNER + nbc:])
    dskip = jnp.repeat(ssm_d, SSM_HEAD_DIM).reshape(1, SSM_INNER)
    yssm = _ssd(proj, dt, cw_x, cw_b, cw_c, cb_x, cb_b, cb_c,
                _group_rows(dt_bias_f, dt_bias_b), _group_rows(a_log_f, a_log_b),
                dskip, ssm_norm_g.reshape(1, SSM_INNER), bsz, seq)

    tm4 = min(512, m)
    x1 = _merge(yret, yssm, proj, x2, w_ret_o.astype(BF16), w_ssm_o.astype(BF16),
                w_out.astype(BF16), tm4)
    return _mlp(x1, norm_mlp_g.reshape(1, D_MODEL), w_mlp_up.astype(BF16),
                w_mlp_down.astype(BF16), g_final, tm4)


def kernel(x, positions, norm_mix_g, w_in, ret_gn_g, w_ret_o, conv_w, conv_b, dt_bias_f, dt_bias_b,
           a_log_f, a_log_b, ssm_d, ssm_norm_g, w_ssm_o, w_out, norm_mlp_g, w_mlp_up, w_mlp_down,
           norm_final_g):
    bsz, seq, _ = x.shape
    depth = w_in.shape[0]
    assert depth == 1, "the final-norm fusion in the MLP call assumes a single layer"
    x2 = x.reshape(bsz * seq, D_MODEL)
    pos2 = jnp.broadcast_to(positions.reshape(bsz * seq, 1).astype(F32), (bsz * seq, ROPE_HALF))
    out = _layer(x2, pos2, bsz, seq, norm_mix_g[0], w_in[0], ret_gn_g[0], w_ret_o[0], conv_w[0],
                 conv_b[0], dt_bias_f[0], dt_bias_b[0], a_log_f[0], a_log_b[0], ssm_d[0],
                 ssm_norm_g[0], w_ssm_o[0], w_out[0], norm_mlp_g[0], w_mlp_up[0], w_mlp_down[0],
                 norm_final_g.reshape(1, D_MODEL))
    return out.reshape(bsz, seq, D_MODEL)
```

```python
import functools

import jax
import jax.numpy as jnp
from jax import lax
from jax.experimental import pallas as pl
from jax.experimental.pallas import tpu as pltpu

F32 = jnp.float32
BF16 = jnp.bfloat16

D_MODEL = 1024
RET_HEADS = 4
RET_QK_DIM = 256
RET_V_DIM = 512
RET_QK = RET_HEADS * RET_QK_DIM
RET_V = RET_HEADS * RET_V_DIM
ROPE_BASE = 10000.0
ROPE_HALF = RET_QK_DIM // 2
SSM_INNER = 2 * D_MODEL
SSM_HEAD_DIM = 64
SSM_HEADS = SSM_INNER // SSM_HEAD_DIM
SSM_GROUPS = 4
SSM_HPG = SSM_HEADS // SSM_GROUPS
SSM_STATE = 128
SSM_CONV = 5
SSM_GROUP_W = SSM_HPG * SSM_HEAD_DIM
D_FF = 4 * D_MODEL
EPS = 1e-6
CHUNK = 128
RET_CHUNK = 256

LANES = 128
BF16_ROWS = 16
MXU_K = 256
CONV_PAD = BF16_ROWS
CONV_K = MXU_K
CONV_SIDE_TAPS = tuple(t for t in range(SSM_CONV) if t != SSM_CONV // 2)
PACK_LANES = 2 * SSM_HPG
E_ORIGIN = LANES - PACK_LANES
LOG2E = 1.4426950408889634

COL_Q = 0
COL_K = COL_Q + RET_QK
COL_V = COL_K + RET_QK
COL_G = COL_V + RET_V
COL_Z = COL_G + RET_V
COL_X = COL_Z + SSM_INNER
COL_B = COL_X + SSM_INNER
COL_C = COL_B + SSM_GROUPS * SSM_STATE
COL_GATE_RET = COL_C + SSM_GROUPS * SSM_STATE
COL_GATE_SSM = COL_GATE_RET + D_MODEL
PROJ_W = COL_GATE_SSM + D_MODEL
INPROJ_TN = 2 * RET_QK

VMEM_LIMIT = 56 * 1024 * 1024


def _cparams(sem):
    return pltpu.CompilerParams(dimension_semantics=sem, vmem_limit_bytes=VMEM_LIMIT)


def _sigmoid(x):
    return 0.5 * jnp.tanh(0.5 * x) + 0.5


def _silu(x):
    return x * _sigmoid(x)


def _rope_store(acc, c, s, scale, proj_ref, col0):
    for hd in range(RET_HEADS):
        lo = hd * RET_QK_DIM
        t1 = acc[:, lo:lo + ROPE_HALF]
        t2 = acc[:, lo + ROPE_HALF:lo + RET_QK_DIM]
        o = col0 + lo
        proj_ref[:, o:o + ROPE_HALF] = ((t1 * c - t2 * s) * scale).astype(BF16)
        proj_ref[:, o + ROPE_HALF:o + RET_QK_DIM] = ((t1 * s + t2 * c) * scale).astype(BF16)


TAIL_STEP = 1


def _inproj_col_tile(j, n_full):
    return jnp.where(j == 0, 0, jnp.where(j == TAIL_STEP, n_full, j - 1))


def _inproj_kernel(x_ref, pos_ref, inv_ref, g_ref, w_ref, wdt_ref, proj_ref, dt_ref, h_sc, *, tail):
    j = pl.program_id(1)

    @pl.when(j == 0)
    def _():
        x = x_ref[...]
        ms = jnp.mean(x * x, axis=-1, keepdims=True)
        h = ((x * lax.rsqrt(ms + EPS)) * g_ref[...]).astype(BF16)
        h_sc[...] = h
        dt_ref[...] = jnp.dot(h, wdt_ref[...], preferred_element_type=F32)
        ang = pos_ref[...] * inv_ref[...]
        c = jnp.cos(ang)
        s = jnp.sin(ang)
        q = jnp.dot(h, w_ref[:, :RET_QK], preferred_element_type=F32)
        _rope_store(q, c, s, 1.0, proj_ref, COL_Q)
        k = jnp.dot(h, w_ref[:, RET_QK:2 * RET_QK], preferred_element_type=F32)
        _rope_store(k, c, s, RET_QK_DIM ** -0.5, proj_ref, COL_K)

    @pl.when(j > TAIL_STEP)
    def _():
        proj_ref[...] = jnp.dot(h_sc[...], w_ref[...], preferred_element_type=F32).astype(BF16)

    @pl.when(j == TAIL_STEP)
    def _():
        proj_ref[:, :tail] = jnp.dot(h_sc[...], w_ref[:, :tail],
                                     preferred_element_type=F32).astype(BF16)


def _inproj(x2, pos2, inv, g, w_main, w_dt, tm):
    m = x2.shape[0]
    tn = INPROJ_TN
    n_full, tail = divmod(PROJ_W, tn)
    assert COL_K + RET_QK == tn and tail > 0 and tail % LANES == 0 and n_full > TAIL_STEP
    col = functools.partial(_inproj_col_tile, n_full=n_full)
    return pl.pallas_call(
        functools.partial(_inproj_kernel, tail=tail),
        grid=(m // tm, n_full + 1),
        in_specs=[
            pl.BlockSpec((tm, D_MODEL), lambda i, j: (i, 0)),
            pl.BlockSpec((tm, ROPE_HALF), lambda i, j: (i, 0)),
            pl.BlockSpec((1, ROPE_HALF), lambda i, j: (0, 0)),
            pl.BlockSpec((1, D_MODEL), lambda i, j: (0, 0)),
            pl.BlockSpec((D_MODEL, tn), lambda i, j: (0, col(j))),
            pl.BlockSpec((D_MODEL, SSM_GROUPS * LANES), lambda i, j: (0, 0)),
        ],
        out_specs=[
            pl.BlockSpec((tm, tn), lambda i, j: (i, col(j))),
            pl.BlockSpec((tm, SSM_GROUPS * LANES), lambda i, j: (i, 0)),
        ],
        out_shape=[
            jax.ShapeDtypeStruct((m, PROJ_W), BF16),
            jax.ShapeDtypeStruct((m, SSM_GROUPS * LANES), F32),
        ],
        scratch_shapes=[
            pltpu.VMEM((tm, D_MODEL), BF16),
        ],
        compiler_params=_cparams(("parallel", "arbitrary")),
        name="inproj",
    )(x2, pos2, inv, g, w_main, w_dt)


def _ret_kernel(logg_ref, q_ref, k_ref, v_ref, o_ref, sb_sc, st_sc, *, nc):
    lg = logg_ref[pl.program_id(1)]
    pos = lax.broadcasted_iota(jnp.int32, (RET_CHUNK, RET_QK_DIM), 0).astype(F32)
    qdec_f = jnp.exp((pos + 1.0) * lg)
    kdec_f = jnp.exp((RET_CHUNK - 1.0 - pos) * lg)
    qdec_b = jnp.exp((RET_CHUNK - pos) * lg)
    kdec_b = jnp.exp(pos * lg)
    chunk_dec = jnp.exp(jnp.full((1, RET_V_DIM), float(RET_CHUNK), F32) * lg)
    ii = lax.broadcasted_iota(jnp.int32, (RET_CHUNK, RET_CHUNK), 0)
    jj = lax.broadcasted_iota(jnp.int32, (RET_CHUNK, RET_CHUNK), 1)
    intra_dec = jnp.exp(jnp.abs(ii - jj).astype(F32) * lg)
    tdims = (((0,), (0,)), ((), ()))

    st_sc[...] = jnp.zeros_like(st_sc)

    def sweep_back(t, carry):
        c = nc - 1 - t
        rows = pl.ds(pl.multiple_of(c * RET_CHUNK, RET_CHUNK), RET_CHUNK)
        sb_sc[c] = st_sc[...].astype(BF16)
        kb = (k_ref[rows, :].astype(F32) * kdec_b).astype(BF16)
        upd = lax.dot_general(kb, v_ref[rows, :], tdims, preferred_element_type=F32)
        st_sc[...] = st_sc[...] * chunk_dec + upd
        return carry

    lax.fori_loop(0, nc, sweep_back, 0, unroll=2)

    st_sc[...] = jnp.zeros_like(st_sc)

    def sweep_fwd(c, carry):
        rows = pl.ds(pl.multiple_of(c * RET_CHUNK, RET_CHUNK), RET_CHUNK)
        qb16 = q_ref[rows, :]
        kb16 = k_ref[rows, :]
        vc = v_ref[rows, :]
        qf32 = qb16.astype(F32)
        scores = lax.dot_general(qb16, kb16, (((1,), (1,)), ((), ())), preferred_element_type=F32)
        y = jnp.dot((scores * intra_dec).astype(BF16), vc, preferred_element_type=F32)
        y = y + jnp.dot((qf32 * qdec_f).astype(BF16), st_sc[...].astype(BF16),
                        preferred_element_type=F32)
        y = y + jnp.dot((qf32 * qdec_b).astype(BF16), sb_sc[c], preferred_element_type=F32)
        o_ref[rows, :] = y.astype(BF16)
        kf = (kb16.astype(F32) * kdec_f).astype(BF16)
        upd = lax.dot_general(kf, vc, tdims, preferred_element_type=F32)
        st_sc[...] = st_sc[...] * chunk_dec + upd
        return carry

    lax.fori_loop(0, nc, sweep_fwd, 0, unroll=2)


def _retention(proj, logg, bsz, seq):
    nc = seq // RET_CHUNK
    qb = COL_Q // RET_QK_DIM
    kb = COL_K // RET_QK_DIM
    vb = COL_V // RET_V_DIM
    return pl.pallas_call(
        functools.partial(_ret_kernel, nc=nc),
        grid=(bsz, RET_HEADS),
        in_specs=[
            pl.BlockSpec(memory_space=pltpu.SMEM),
            pl.BlockSpec((seq, RET_QK_DIM), lambda b, h: (b, qb + h)),
            pl.BlockSpec((seq, RET_QK_DIM), lambda b, h: (b, kb + h)),
            pl.BlockSpec((seq, RET_V_DIM), lambda b, h: (b, vb + h)),
        ],
        out_specs=pl.BlockSpec((seq, RET_V_DIM), lambda b, h: (b, h)),
        out_shape=jax.ShapeDtypeStruct((bsz * seq, RET_V), BF16),
        scratch_shapes=[
            pltpu.VMEM((nc, RET_QK_DIM, RET_V_DIM), BF16),
            pltpu.VMEM((RET_QK_DIM, RET_V_DIM), F32),
        ],
        compiler_params=_cparams(("parallel", "arbitrary")),
        name="retention",
    )(logg, proj, proj, proj)


def _ssd_kernel(z_ref, x_ref, b_ref, c_ref, dt_ref, cwx_ref, cwb_ref, cwc_ref,
                cbx_ref, cbb_ref, cbc_ref, bias_ref, alog_ref, dskip_ref, ng_ref,
                o_ref, xs_sc, bm_sc, cm_sc, stb_sc, st_sc, sh_sc, e_sc,
                pcol_sc, prowt_sc, din_sc, wgt_sc, dtp_sc, y_sc, xw_sc, dch_sc, *, nc, pack):
    seq = nc * CHUNK
    hpg = SSM_HPG
    gw = SSM_GROUP_W
    cw = gw + 2 * SSM_STATE
    nblk = nc // pack
    tdims = (((0,), (0,)), ((), ()))

    def chunk_rows(c):
        return pl.ds(pl.multiple_of(c * CHUNK, CHUNK), CHUNK)

    si = lax.broadcasted_iota(jnp.int32, (CHUNK, CONV_K), 0)
    sr = lax.broadcasted_iota(jnp.int32, (CHUNK, CONV_K), 1)
    for idx, t in enumerate(CONV_SIDE_TAPS):
        sh_sc[idx] = jnp.where(sr == si + CONV_PAD + t - SSM_CONV // 2, 1.0, 0.0).astype(BF16)
    er = lax.broadcasted_iota(jnp.int32, (2 * LANES, 2 * gw), 0)
    ec = lax.broadcasted_iota(jnp.int32, (2 * LANES, 2 * gw), 1) // SSM_HEAD_DIM
    e_sc[...] = jnp.where(er - E_ORIGIN == ec, 1.0, 0.0).astype(BF16)

    def expand(v, slot, col0, ncol):
        e = e_sc[pl.ds(pl.multiple_of(E_ORIGIN - 2 * hpg * slot, 2 * hpg), LANES),
                 col0:col0 + ncol]
        return jnp.dot(v.astype(BF16), e, preferred_element_type=F32)

    lane = lax.broadcasted_iota(jnp.int32, (CHUNK, LANES), 1)
    fwd_lane = jnp.bitwise_and(lane, 2 * hpg - 1) < hpg
    first_slot = lane < 2 * hpg
    ii = lax.broadcasted_iota(jnp.int32, (CHUNK, CHUNK), 0)
    jj = lax.broadcasted_iota(jnp.int32, (CHUNK, CHUNK), 1)
    tril = jnp.where(jj <= ii, 1.0, 0.0).astype(F32)
    causal = jj <= ii
    diag = ii == jj
    pair_lo = lane < SSM_HEAD_DIM
    bias_row = bias_ref[0]
    a_row = -jnp.exp(alog_ref[0])

    def pre_body(blk, carry):
        raw = jnp.zeros((CHUNK, LANES), F32)
        for k in range(pack):
            t = jnp.where(first_slot, dt_ref[chunk_rows(blk * pack + k), :], 0.0)
            raw = raw + (pltpu.roll(t, 2 * hpg * k, axis=1) if k else t)
        dtp = jax.nn.softplus(raw + bias_row)
        la = dtp * a_row
        cum = jnp.dot(tril, la, preferred_element_type=F32, precision=lax.Precision.HIGHEST)
        tot = cum[CHUNK - 1:CHUNK, :]
        excl = cum - la
        pcol = jnp.where(fwd_lane, cum, tot - excl)
        pcol_sc[blk] = pcol * LOG2E
        prowt_sc[blk] = ((pcol - jnp.log(dtp)) * LOG2E).T
        din_sc[blk] = jnp.exp(pcol)
        wgt_sc[blk] = jnp.exp(jnp.where(fwd_lane, tot - cum, excl)) * dtp
        dtp_sc[blk] = dtp
        return carry

    lax.fori_loop(0, nblk, pre_body, 0)

    w_all = jnp.concatenate([cwx_ref[...], cwb_ref[...], cwc_ref[...]], axis=1)
    b_all = jnp.concatenate([cbx_ref[...], cbb_ref[...], cbc_ref[...]], axis=1)
    zero16 = jnp.zeros((), BF16)

    def conv_body(c, carry):
        r0 = pl.multiple_of(c * CHUNK, CHUNK)
        prev0 = pl.multiple_of(jnp.maximum(r0 - CONV_PAD, 0), CONV_PAD)
        next0 = pl.multiple_of(jnp.minimum(r0 + CHUNK, seq - CONV_PAD), CONV_PAD)
        pieces = []
        for src_ref in (x_ref, b_ref, c_ref):
            width = src_ref.shape[1]
            pieces.append(jnp.concatenate([
                jnp.where(c > 0, src_ref[pl.ds(prev0, CONV_PAD), :], zero16),
                src_ref[pl.ds(r0, CHUNK), :],
                jnp.where(c < nc - 1, src_ref[pl.ds(next0, CONV_PAD), :], zero16),
                jnp.zeros((CONV_K - CHUNK - 2 * CONV_PAD, width), BF16)], axis=0))
        stage = jnp.concatenate(pieces, axis=1)
        mid = SSM_CONV // 2
        acc = stage[CONV_PAD:CONV_PAD + CHUNK, :].astype(F32) * w_all[mid:mid + 1, :]
        for idx, t in enumerate(CONV_SIDE_TAPS):
            acc = acc + jnp.dot(sh_sc[idx], stage, preferred_element_type=F32) * w_all[t:t + 1, :]
        out = _silu(acc + b_all).astype(BF16)
        xs_sc[pl.ds(r0, CHUNK), :] = out[:, :gw]
        bm_sc[pl.ds(r0, CHUNK), :] = out[:, gw:gw + SSM_STATE]
        cm_sc[pl.ds(r0, CHUNK), :] = out[:, gw + SSM_STATE:]
        return carry

    lax.fori_loop(0, nc, conv_body, 0, unroll=4)

    def stage_back(c, slot_buf):
        blk = c // pack
        slot = c - blk * pack
        lhs = jnp.concatenate([wgt_sc[blk], din_sc[blk, 0:8, :]], axis=0)
        ex = expand(lhs, slot, gw, gw)
        xw_sc[slot_buf] = (xs_sc[chunk_rows(c), :].astype(F32) * ex[:CHUNK, :]).astype(BF16)
        dch_sc[slot_buf] = ex[CHUNK:CHUNK + 8, :]

    st_sc[...] = jnp.zeros_like(st_sc)
    stage_back(nc - 1, 0)

    def sweep_back(t, carry):
        c = nc - 1 - t
        buf = jnp.bitwise_and(t, 1)
        stb_sc[c] = st_sc[...].astype(BF16)
        upd = lax.dot_general(bm_sc[chunk_rows(c), :], xw_sc[buf], tdims,
                              preferred_element_type=F32)
        st_sc[...] = st_sc[...] * dch_sc[buf][0:1, :] + upd
        stage_back(jnp.maximum(c - 1, 0), 1 - buf)
        return carry

    lax.fori_loop(0, nc, sweep_back, 0, unroll=2)

    def mix(c):
        rows = chunk_rows(c)
        blk = c // pack
        slot = c - blk * pack
        unpack = jnp.bitwise_and(LANES - 2 * hpg * slot, LANES - 1)
        pcol = pltpu.roll(pcol_sc[blk], unpack, axis=1)
        prow_t = prowt_sc[blk, pl.ds(pl.multiple_of(2 * hpg * slot, 2 * hpg), 2 * hpg), :]
        xs = xs_sc[rows, :]
        bm = bm_sc[rows, :]
        cm = cm_sc[rows, :]
        cb = lax.dot_general(cm, bm, (((1,), (1,)), ((), ())), preferred_element_type=F32)
        cb_diag = jnp.sum(jnp.where(diag, cb, 0.0), axis=1, keepdims=True)
        ue = expand(jnp.where(fwd_lane, wgt_sc[blk], cb_diag * dtp_sc[blk]), slot, 0, 2 * gw)
        de = expand(din_sc[blk], slot, 0, 2 * gw)
        y_st = (jnp.dot(cm, st_sc[...].astype(BF16), preferred_element_type=F32) * de[:, :gw]
                + jnp.dot(cm, stb_sc[c], preferred_element_type=F32) * de[:, gw:])
        pieces = []
        for pr in range(hpg // 2):
            xs_p = xs[:, pr * LANES:(pr + 1) * LANES]
            m_pair = []
            for hh in (2 * pr, 2 * pr + 1):
                col_f = jnp.broadcast_to(pcol[:, hh:hh + 1], (CHUNK, CHUNK))
                col_b = jnp.broadcast_to(pcol[:, hpg + hh:hpg + hh + 1], (CHUNK, CHUNK))
                arg = jnp.where(causal, col_f - prow_t[hh:hh + 1, :],
                                col_b - prow_t[hpg + hh:hpg + hh + 1, :])
                m_pair.append((cb * jnp.exp2(arg)).astype(BF16))
            x_pair = jnp.concatenate([jnp.where(pair_lo, xs_p, zero16),
                                      jnp.where(pair_lo, zero16, xs_p)], axis=0)
            pieces.append(jnp.dot(jnp.concatenate(m_pair, axis=1), x_pair,
                                  preferred_element_type=F32))
        xs32 = xs.astype(F32)
        y_sc[...] = jnp.concatenate(pieces, axis=1) + y_st + (dskip_ref[...] + ue[:, gw:]) * xs32
        xw = (xs32 * ue[:, :gw]).astype(BF16)
        upd = lax.dot_general(bm, xw, tdims, preferred_element_type=F32)
        st_sc[...] = st_sc[...] * de[CHUNK - 1:CHUNK, :gw] + upd

    def finish(c):
        rows = chunk_rows(c)
        y = y_sc[...] * _silu(z_ref[rows, :].astype(F32))
        y = y * lax.rsqrt(jnp.mean(y * y, axis=-1, keepdims=True) + EPS)
        o_ref[rows, :] = (y * ng_ref[...]).astype(BF16)

    st_sc[...] = jnp.zeros_like(st_sc)
    mix(0)

    def sweep_fwd(c, carry):
        finish(c - 1)
        mix(c)
        return carry

    lax.fori_loop(1, nc, sweep_fwd, 0, unroll=2)
    finish(nc - 1)


def _ssd(proj, dt, cw_x, cw_b, cw_c, cb_x, cb_b, cb_c, bias_grp, alog_grp, dskip, norm_g, bsz, seq):
    nc = seq // CHUNK
    zb = COL_Z // SSM_GROUP_W
    xb = COL_X // SSM_GROUP_W
    bb = COL_B // SSM_STATE
    cb = COL_C // SSM_STATE
    gw = SSM_GROUP_W
    pack = min(LANES // PACK_LANES, nc)
    assert nc % pack == 0
    nblk = nc // pack
    packed = pltpu.VMEM((nblk, CHUNK, LANES), F32)
    return pl.pallas_call(
        functools.partial(_ssd_kernel, nc=nc, pack=pack),
        grid=(bsz, SSM_GROUPS),
        in_specs=[
            pl.BlockSpec((seq, gw), lambda b, g: (b, zb + g)),
            pl.BlockSpec((seq, gw), lambda b, g: (b, xb + g)),
            pl.BlockSpec((seq, SSM_STATE), lambda b, g: (b, bb + g)),
            pl.BlockSpec((seq, SSM_STATE), lambda b, g: (b, cb + g)),
            pl.BlockSpec((seq, LANES), lambda b, g: (b, g)),
            pl.BlockSpec((SSM_CONV, gw), lambda b, g: (0, g)),
            pl.BlockSpec((SSM_CONV, SSM_STATE), lambda b, g: (0, g)),
            pl.BlockSpec((SSM_CONV, SSM_STATE), lambda b, g: (0, g)),
            pl.BlockSpec((1, gw), lambda b, g: (0, g)),
            pl.BlockSpec((1, SSM_STATE), lambda b, g: (0, g)),
            pl.BlockSpec((1, SSM_STATE), lambda b, g: (0, g)),
            pl.BlockSpec((1, 1, LANES), lambda b, g: (g, 0, 0)),
            pl.BlockSpec((1, 1, LANES), lambda b, g: (g, 0, 0)),
            pl.BlockSpec((1, gw), lambda b, g: (0, g)),
            pl.BlockSpec((1, gw), lambda b, g: (0, g)),
        ],
        out_specs=pl.BlockSpec((seq, gw), lambda b, g: (b, g)),
        out_shape=jax.ShapeDtypeStruct((bsz * seq, SSM_INNER), BF16),
        scratch_shapes=[
            pltpu.VMEM((seq, gw), BF16),
            pltpu.VMEM((seq, SSM_STATE), BF16),
            pltpu.VMEM((seq, SSM_STATE), BF16),
            pltpu.VMEM((nc, SSM_STATE, gw), BF16),
            pltpu.VMEM((SSM_STATE, gw), F32),
            pltpu.VMEM((len(CONV_SIDE_TAPS), CHUNK, CONV_K), BF16),
            pltpu.VMEM((2 * LANES, 2 * gw), BF16),
            packed, packed, packed, packed, packed,
            pltpu.VMEM((CHUNK, gw), F32),
            pltpu.VMEM((2, CHUNK, gw), BF16),
            pltpu.VMEM((2, 8, gw), F32),
        ],
        compiler_params=_cparams(("parallel", "arbitrary")),
        name="ssd",
    )(proj, proj, proj, proj, dt, cw_x, cw_b, cw_c, cb_x, cb_b, cb_c, bias_grp, alog_grp,
      dskip, norm_g)


def _merge_kernel(yr_ref, g_ref, gn_ref, ys_ref, gr_ref, gs_ref, x_ref, wr_ref, ws_ref, wo_ref,
                  o_ref):
    y_ret = None
    for hd in range(RET_HEADS):
        cols = slice(hd * RET_V_DIM, (hd + 1) * RET_V_DIM)
        y = yr_ref[:, cols].astype(F32)
        mu = jnp.mean(y, axis=-1, keepdims=True)
        d = y - mu
        var = jnp.mean(d * d, axis=-1, keepdims=True)
        yn = (d * lax.rsqrt(var + EPS)) * gn_ref[:, cols]
        act = (yn * _silu(g_ref[:, cols].astype(F32))).astype(BF16)
        part = jnp.dot(act, wr_ref[cols, :], preferred_element_type=F32)
        y_ret = part if y_ret is None else y_ret + part
    y_ssm = jnp.dot(ys_ref[...], ws_ref[...], preferred_element_type=F32)
    mixed = (_sigmoid(gr_ref[...].astype(F32)) * y_ret
             + _sigmoid(gs_ref[...].astype(F32)) * y_ssm)
    o_ref[...] = x_ref[...] + jnp.dot(mixed.astype(BF16), wo_ref[...], preferred_element_type=F32)


def _resident(shape):
    return pl.BlockSpec(shape, lambda i: (0,) * len(shape), pipeline_mode=pl.Buffered(1))


def _merge(yret, yssm, proj, x2, gn_g, w_ret_o, w_ssm_o, w_out, tm):
    m = x2.shape[0]
    gb = COL_G // RET_V
    grb = COL_GATE_RET // D_MODEL
    gsb = COL_GATE_SSM // D_MODEL
    return pl.pallas_call(
        _merge_kernel,
        grid=(m // tm,),
        in_specs=[
            pl.BlockSpec((tm, RET_V), lambda i: (i, 0)),
            pl.BlockSpec((tm, RET_V), lambda i: (i, gb)),
            _resident((1, RET_V)),
            pl.BlockSpec((tm, SSM_INNER), lambda i: (i, 0)),
            pl.BlockSpec((tm, D_MODEL), lambda i: (i, grb)),
            pl.BlockSpec((tm, D_MODEL), lambda i: (i, gsb)),
            pl.BlockSpec((tm, D_MODEL), lambda i: (i, 0)),
            _resident((RET_V, D_MODEL)),
            _resident((SSM_INNER, D_MODEL)),
            _resident((D_MODEL, D_MODEL)),
        ],
        out_specs=pl.BlockSpec((tm, D_MODEL), lambda i: (i, 0)),
        out_shape=jax.ShapeDtypeStruct((m, D_MODEL), F32),
        compiler_params=_cparams(("parallel",)),
        name="merge",
    )(yret, proj, gn_g, yssm, proj, proj, x2, w_ret_o, w_ssm_o, w_out)


def _rms(x, g):
    return (x * lax.rsqrt(jnp.mean(x * x, axis=-1, keepdims=True) + EPS)) * g


def _mlp_kernel(x_ref, gm_ref, wu_ref, wd_ref, gf_ref, o_ref):
    x = x_ref[...]
    h = _rms(x, gm_ref[...]).astype(BF16)
    up = jnp.dot(h, wu_ref[...], preferred_element_type=F32)
    act = jnp.square(jnp.maximum(up, 0.0)).astype(BF16)
    x2 = x + jnp.dot(act, wd_ref[...], preferred_element_type=F32)
    o_ref[...] = _rms(x2, gf_ref[...])


def _mlp(x1, g_mlp, w_up, w_down, g_final, tm):
    m = x1.shape[0]
    return pl.pallas_call(
        _mlp_kernel,
        grid=(m // tm,),
        in_specs=[
            pl.BlockSpec((tm, D_MODEL), lambda i: (i, 0)),
            _resident((1, D_MODEL)),
            _resident((D_MODEL, D_FF)),
            _resident((D_FF, D_MODEL)),
            _resident((1, D_MODEL)),
        ],
        out_specs=pl.BlockSpec((tm, D_MODEL), lambda i: (i, 0)),
        out_shape=jax.ShapeDtypeStruct((m, D_MODEL), F32),
        compiler_params=_cparams(("parallel",)),
        name="mlp",
    )(x1, g_mlp, w_up, w_down, g_final)


def _group_rows(fwd, bwd):
    f = fwd.reshape(SSM_GROUPS, SSM_HPG)
    b = bwd.reshape(SSM_GROUPS, SSM_HPG)
    one = jnp.concatenate([f, b], axis=1)
    return jnp.tile(one, (1, LANES // PACK_LANES)).reshape(SSM_GROUPS, 1, LANES)


def _layer(x2, pos2, bsz, seq, norm_mix_g, w_in, ret_gn_g, w_ret_o, conv_w, conv_b, dt_bias_f,
           dt_bias_b, a_log_f, a_log_b, ssm_d, ssm_norm_g, w_ssm_o, w_out, norm_mlp_g, w_mlp_up,
           w_mlp_down, g_final):
    m = bsz * seq
    dt0 = COL_C + SSM_GROUPS * SSM_STATE
    w_main = jnp.concatenate([w_in[:, :dt0], w_in[:, dt0 + 2 * SSM_HEADS:]], axis=1).astype(BF16)
    wdt = w_in[:, dt0:dt0 + 2 * SSM_HEADS]
    wdt_f = wdt[:, :SSM_HEADS].reshape(D_MODEL, SSM_GROUPS, SSM_HPG)
    wdt_b = wdt[:, SSM_HEADS:].reshape(D_MODEL, SSM_GROUPS, SSM_HPG)
    wdt_pad = jnp.zeros((D_MODEL, SSM_GROUPS, LANES - 2 * SSM_HPG), F32)
    w_dt = jnp.concatenate([wdt_f, wdt_b, wdt_pad], axis=2).reshape(D_MODEL, SSM_GROUPS * LANES)
    w_dt = w_dt.astype(BF16)

    inv = (ROPE_BASE ** (-jnp.arange(ROPE_HALF, dtype=F32) / ROPE_HALF)).reshape(1, ROPE_HALF)
    tm1 = min(1024, m)
    proj, dt = _inproj(x2, pos2, inv, norm_mix_g.reshape(1, D_MODEL), w_main, w_dt, tm1)

    logg = jnp.log1p(-jnp.exp2(-5.0 - jnp.arange(RET_HEADS, dtype=F32)))
    yret = _retention(proj, logg, bsz, seq)

    nbc = SSM_GROUPS * SSM_STATE
    cw_x, cw_b, cw_c = (conv_w[:, :SSM_INNER], conv_w[:, SSM_INNER:SSM_INNER + nbc],
                        conv_w[:, SSM_INNER + nbc:])
    cb2 = conv_b.reshape(1, -1)
    cb_x, cb_b, cb_c = (cb2[:, :SSM_INNER], cb2[:, SSM_INNER:SSM_INNER + nbc],
                        cb2[:, SSM_INNER + nbc:])
    dskip = jnp.repeat(ssm_d, SSM_HEAD_DIM).reshape(1, SSM_INNER)
    yssm = _ssd(proj, dt, cw_x, cw_b, cw_c, cb_x, cb_b, cb_c,
                _group_rows(dt_bias_f, dt_bias_b), _group_rows(a_log_f, a_log_b),
                dskip, ssm_norm_g.reshape(1, SSM_INNER), bsz, seq)

    tm4 = min(512, m)
    x1 = _merge(yret, yssm, proj, x2, ret_gn_g.reshape(1, RET_V), w_ret_o.astype(BF16),
                w_ssm_o.astype(BF16), w_out.astype(BF16), tm4)
    return _mlp(x1, norm_mlp_g.reshape(1, D_MODEL), w_mlp_up.astype(BF16),
                w_mlp_down.astype(BF16), g_final, tm4)


def kernel(x, positions, norm_mix_g, w_in, ret_gn_g, w_ret_o, conv_w, conv_b, dt_bias_f, dt_bias_b,
           a_log_f, a_log_b, ssm_d, ssm_norm_g, w_ssm_o, w_out, norm_mlp_g, w_mlp_up, w_mlp_down,
           norm_final_g):
    bsz, seq, _ = x.shape
    depth = w_in.shape[0]
    assert depth == 1, "the final-norm fusion in the MLP call assumes a single layer"
    x2 = x.reshape(bsz * seq, D_MODEL)
    pos2 = jnp.broadcast_to(positions.reshape(bsz * seq, 1).astype(F32), (bsz * seq, ROPE_HALF))
    out = _layer(x2, pos2, bsz, seq, norm_mix_g[0], w_in[0], ret_gn_g[0], w_ret_o[0], conv_w[0],
                 conv_b[0], dt_bias_f[0], dt_bias_b[0], a_log_f[0], a_log_b[0], ssm_d[0],
                 ssm_norm_g[0], w_ssm_o[0], w_out[0], norm_mlp_g[0], w_mlp_up[0], w_mlp_down[0],
                 norm_final_g.reshape(1, D_MODEL))
    return out.reshape(bsz, seq, D_MODEL)
```

```python
import functools

import jax
import jax.numpy as jnp
from jax import lax
from jax.experimental import pallas as pl
from jax.experimental.pallas import tpu as pltpu

F32 = jnp.float32
BF16 = jnp.bfloat16

D_MODEL = 1024
RET_HEADS = 4
RET_QK_DIM = 256
RET_V_DIM = 512
RET_QK = RET_HEADS * RET_QK_DIM
RET_V = RET_HEADS * RET_V_DIM
ROPE_BASE = 10000.0
ROPE_HALF = RET_QK_DIM // 2
SSM_INNER = 2 * D_MODEL
SSM_HEAD_DIM = 64
SSM_HEADS = SSM_INNER // SSM_HEAD_DIM
SSM_GROUPS = 4
SSM_HPG = SSM_HEADS // SSM_GROUPS
SSM_STATE = 128
SSM_CONV = 5
SSM_GROUP_W = SSM_HPG * SSM_HEAD_DIM
D_FF = 4 * D_MODEL
EPS = 1e-6
CHUNK = 128
RET_CHUNK = 256

LANES = 128
BF16_ROWS = 16
MXU_K = 256
CONV_PAD = BF16_ROWS
CONV_K = MXU_K
CONV_SIDE_TAPS = tuple(t for t in range(SSM_CONV) if t != SSM_CONV // 2)
PACK_LANES = 2 * SSM_HPG
E_ORIGIN = LANES - PACK_LANES
LOG2E = 1.4426950408889634

COL_Q = 0
COL_K = COL_Q + RET_QK
COL_V = COL_K + RET_QK
COL_G = COL_V + RET_V
COL_Z = COL_G + RET_V
COL_X = COL_Z + SSM_INNER
COL_B = COL_X + SSM_INNER
COL_C = COL_B + SSM_GROUPS * SSM_STATE
COL_GATE_RET = COL_C + SSM_GROUPS * SSM_STATE
COL_GATE_SSM = COL_GATE_RET + D_MODEL
PROJ_W = COL_GATE_SSM + D_MODEL
INPROJ_TN = 2 * RET_QK

VMEM_LIMIT = 56 * 1024 * 1024


def _cparams(sem):
    return pltpu.CompilerParams(dimension_semantics=sem, vmem_limit_bytes=VMEM_LIMIT)


def _sigmoid(x):
    return 0.5 * jnp.tanh(0.5 * x) + 0.5


def _silu(x):
    return x * _sigmoid(x)


def _rope_store(acc, c, s, scale, proj_ref, col0):
    for hd in range(RET_HEADS):
        lo = hd * RET_QK_DIM
        t1 = acc[:, lo:lo + ROPE_HALF]
        t2 = acc[:, lo + ROPE_HALF:lo + RET_QK_DIM]
        o = col0 + lo
        proj_ref[:, o:o + ROPE_HALF] = ((t1 * c - t2 * s) * scale).astype(BF16)
        proj_ref[:, o + ROPE_HALF:o + RET_QK_DIM] = ((t1 * s + t2 * c) * scale).astype(BF16)


TAIL_STEP = 1


def _inproj_col_tile(j, n_full):
    return jnp.where(j == 0, 0, jnp.where(j == TAIL_STEP, n_full, j - 1))


def _inproj_kernel(x_ref, pos_ref, inv_ref, g_ref, w_ref, wdt_ref, proj_ref, dt_ref, h_sc, *, tail):
    j = pl.program_id(1)

    @pl.when(j == 0)
    def _():
        x = x_ref[...]
        ms = jnp.mean(x * x, axis=-1, keepdims=True)
        h = ((x * lax.rsqrt(ms + EPS)) * g_ref[...]).astype(BF16)
        h_sc[...] = h
        dt_ref[...] = jnp.dot(h, wdt_ref[...], preferred_element_type=F32)
        ang = pos_ref[...] * inv_ref[...]
        c = jnp.cos(ang)
        s = jnp.sin(ang)
        q = jnp.dot(h, w_ref[:, :RET_QK], preferred_element_type=F32)
        _rope_store(q, c, s, 1.0, proj_ref, COL_Q)
        k = jnp.dot(h, w_ref[:, RET_QK:2 * RET_QK], preferred_element_type=F32)
        _rope_store(k, c, s, RET_QK_DIM ** -0.5, proj_ref, COL_K)

    @pl.when(j > TAIL_STEP)
    def _():
        proj_ref[...] = jnp.dot(h_sc[...], w_ref[...], preferred_element_type=F32).astype(BF16)

    @pl.when(j == TAIL_STEP)
    def _():
        proj_ref[:, :tail] = jnp.dot(h_sc[...], w_ref[:, :tail],
                                     preferred_element_type=F32).astype(BF16)


def _inproj(x2, pos2, inv, g, w_main, w_dt, tm):
    m = x2.shape[0]
    tn = INPROJ_TN
    n_full, tail = divmod(PROJ_W, tn)
    assert COL_K + RET_QK == tn and tail > 0 and tail % LANES == 0 and n_full > TAIL_STEP
    col = functools.partial(_inproj_col_tile, n_full=n_full)
    return pl.pallas_call(
        functools.partial(_inproj_kernel, tail=tail),
        grid=(m // tm, n_full + 1),
        in_specs=[
            pl.BlockSpec((tm, D_MODEL), lambda i, j: (i, 0)),
            pl.BlockSpec((tm, ROPE_HALF), lambda i, j: (i, 0)),
            pl.BlockSpec((1, ROPE_HALF), lambda i, j: (0, 0)),
            pl.BlockSpec((1, D_MODEL), lambda i, j: (0, 0)),
            pl.BlockSpec((D_MODEL, tn), lambda i, j: (0, col(j))),
            pl.BlockSpec((D_MODEL, SSM_GROUPS * LANES), lambda i, j: (0, 0)),
        ],
        out_specs=[
            pl.BlockSpec((tm, tn), lambda i, j: (i, col(j))),
            pl.BlockSpec((tm, SSM_GROUPS * LANES), lambda i, j: (i, 0)),
        ],
        out_shape=[
            jax.ShapeDtypeStruct((m, PROJ_W), BF16),
            jax.ShapeDtypeStruct((m, SSM_GROUPS * LANES), F32),
        ],
        scratch_shapes=[
            pltpu.VMEM((tm, D_MODEL), BF16),
        ],
        compiler_params=_cparams(("parallel", "arbitrary")),
        name="inproj",
    )(x2, pos2, inv, g, w_main, w_dt)


def _ret_kernel(logg_ref, q_ref, k_ref, v_ref, o_ref, sb_sc, st_sc, *, nc):
    lg = logg_ref[pl.program_id(1)]
    pos = lax.broadcasted_iota(jnp.int32, (RET_CHUNK, RET_QK_DIM), 0).astype(F32)
    qdec_f = jnp.exp((pos + 1.0) * lg)
    kdec_f = jnp.exp((RET_CHUNK - 1.0 - pos) * lg)
    qdec_b = jnp.exp((RET_CHUNK - pos) * lg)
    kdec_b = jnp.exp(pos * lg)
    chunk_dec = jnp.exp(jnp.full((1, RET_V_DIM), float(RET_CHUNK), F32) * lg)
    ii = lax.broadcasted_iota(jnp.int32, (RET_CHUNK, RET_CHUNK), 0)
    jj = lax.broadcasted_iota(jnp.int32, (RET_CHUNK, RET_CHUNK), 1)
    intra_dec = jnp.exp(jnp.abs(ii - jj).astype(F32) * lg)
    tdims = (((0,), (0,)), ((), ()))

    st_sc[...] = jnp.zeros_like(st_sc)

    def sweep_back(t, carry):
        c = nc - 1 - t
        rows = pl.ds(pl.multiple_of(c * RET_CHUNK, RET_CHUNK), RET_CHUNK)
        sb_sc[c] = st_sc[...].astype(BF16)
        kb = (k_ref[rows, :].astype(F32) * kdec_b).astype(BF16)
        upd = lax.dot_general(kb, v_ref[rows, :], tdims, preferred_element_type=F32)
        st_sc[...] = st_sc[...] * chunk_dec + upd
        return carry

    lax.fori_loop(0, nc, sweep_back, 0, unroll=2)

    st_sc[...] = jnp.zeros_like(st_sc)

    def sweep_fwd(c, carry):
        rows = pl.ds(pl.multiple_of(c * RET_CHUNK, RET_CHUNK), RET_CHUNK)
        qb16 = q_ref[rows, :]
        kb16 = k_ref[rows, :]
        vc = v_ref[rows, :]
        qf32 = qb16.astype(F32)
        scores = lax.dot_general(qb16, kb16, (((1,), (1,)), ((), ())), preferred_element_type=F32)
        y = jnp.dot((scores * intra_dec).astype(BF16), vc, preferred_element_type=F32)
        y = y + jnp.dot((qf32 * qdec_f).astype(BF16), st_sc[...].astype(BF16),
                        preferred_element_type=F32)
        y = y + jnp.dot((qf32 * qdec_b).astype(BF16), sb_sc[c], preferred_element_type=F32)
        o_ref[rows, :] = y.astype(BF16)
        kf = (kb16.astype(F32) * kdec_f).astype(BF16)
        upd = lax.dot_general(kf, vc, tdims, preferred_element_type=F32)
        st_sc[...] = st_sc[...] * chunk_dec + upd
        return carry

    lax.fori_loop(0, nc, sweep_fwd, 0, unroll=2)


def _retention(proj, logg, bsz, seq):
    nc = seq // RET_CHUNK
    qb = COL_Q // RET_QK_DIM
    kb = COL_K // RET_QK_DIM
    vb = COL_V // RET_V_DIM
    return pl.pallas_call(
        functools.partial(_ret_kernel, nc=nc),
        grid=(bsz, RET_HEADS),
        in_specs=[
            pl.BlockSpec(memory_space=pltpu.SMEM),
            pl.BlockSpec((seq, RET_QK_DIM), lambda b, h: (b, qb + h)),
            pl.BlockSpec((seq, RET_QK_DIM), lambda b, h: (b, kb + h)),
            pl.BlockSpec((seq, RET_V_DIM), lambda b, h: (b, vb + h)),
        ],
        out_specs=pl.BlockSpec((seq, RET_V_DIM), lambda b, h: (b, h)),
        out_shape=jax.ShapeDtypeStruct((bsz * seq, RET_V), BF16),
        scratch_shapes=[
            pltpu.VMEM((nc, RET_QK_DIM, RET_V_DIM), BF16),
            pltpu.VMEM((RET_QK_DIM, RET_V_DIM), F32),
        ],
        compiler_params=_cparams(("parallel", "arbitrary")),
        name="retention",
    )(logg, proj, proj, proj)


def _ssd_kernel(z_ref, x_ref, b_ref, c_ref, dt_ref, cwx_ref, cwb_ref, cwc_ref,
                cbx_ref, cbb_ref, cbc_ref, bias_ref, alog_ref, dskip_ref, ng_ref,
                o_ref, xs_sc, bm_sc, cm_sc, stb_sc, st_sc, sh_sc, e_sc,
                pcol_sc, prowt_sc, din_sc, wgt_sc, dtp_sc, y_sc, xw_sc, dch_sc, *, nc, pack):
    seq = nc * CHUNK
    hpg = SSM_HPG
    gw = SSM_GROUP_W
    cw = gw + 2 * SSM_STATE
    nblk = nc // pack
    tdims = (((0,), (0,)), ((), ()))

    def chunk_rows(c):
        return pl.ds(pl.multiple_of(c * CHUNK, CHUNK), CHUNK)

    si = lax.broadcasted_iota(jnp.int32, (CHUNK, CONV_K), 0)
    sr = lax.broadcasted_iota(jnp.int32, (CHUNK, CONV_K), 1)
    for idx, t in enumerate(CONV_SIDE_TAPS):
        sh_sc[idx] = jnp.where(sr == si + CONV_PAD + t - SSM_CONV // 2, 1.0, 0.0).astype(BF16)
    er = lax.broadcasted_iota(jnp.int32, (2 * LANES, 2 * gw), 0)
    ec = lax.broadcasted_iota(jnp.int32, (2 * LANES, 2 * gw), 1) // SSM_HEAD_DIM
    e_sc[...] = jnp.where(er - E_ORIGIN == ec, 1.0, 0.0).astype(BF16)

    def expand(v, slot, col0, ncol):
        e = e_sc[pl.ds(pl.multiple_of(E_ORIGIN - 2 * hpg * slot, 2 * hpg), LANES),
                 col0:col0 + ncol]
        return jnp.dot(v.astype(BF16), e, preferred_element_type=F32)

    lane = lax.broadcasted_iota(jnp.int32, (CHUNK, LANES), 1)
    fwd_lane = jnp.bitwise_and(lane, 2 * hpg - 1) < hpg
    first_slot = lane < 2 * hpg
    ii = lax.broadcasted_iota(jnp.int32, (CHUNK, CHUNK), 0)
    jj = lax.broadcasted_iota(jnp.int32, (CHUNK, CHUNK), 1)
    tril = jnp.where(jj <= ii, 1.0, 0.0).astype(F32)
    causal = jj <= ii
    diag = ii == jj
    pair_lo = lane < SSM_HEAD_DIM
    bias_row = bias_ref[0]
    a_row = -jnp.exp(alog_ref[0])

    def pre_body(blk, carry):
        raw = jnp.zeros((CHUNK, LANES), F32)
        for k in range(pack):
            t = jnp.where(first_slot, dt_ref[chunk_rows(blk * pack + k), :], 0.0)
            raw = raw + (pltpu.roll(t, 2 * hpg * k, axis=1) if k else t)
        dtp = jax.nn.softplus(raw + bias_row)
        la = dtp * a_row
        cum = jnp.dot(tril, la, preferred_element_type=F32, precision=lax.Precision.HIGHEST)
        tot = cum[CHUNK - 1:CHUNK, :]
        excl = cum - la
        pcol = jnp.where(fwd_lane, cum, tot - excl)
        pcol_sc[blk] = pcol * LOG2E
        prowt_sc[blk] = ((pcol - jnp.log(dtp)) * LOG2E).T
        din_sc[blk] = jnp.exp(pcol)
        wgt_sc[blk] = jnp.exp(jnp.where(fwd_lane, tot - cum, excl)) * dtp
        dtp_sc[blk] = dtp
        return carry

    lax.fori_loop(0, nblk, pre_body, 0)

    w_all = jnp.concatenate([cwx_ref[...], cwb_ref[...], cwc_ref[...]], axis=1)
    b_all = jnp.concatenate([cbx_ref[...], cbb_ref[...], cbc_ref[...]], axis=1)
    zero16 = jnp.zeros((), BF16)

    def conv_body(c, carry):
        r0 = pl.multiple_of(c * CHUNK, CHUNK)
        prev0 = pl.multiple_of(jnp.maximum(r0 - CONV_PAD, 0), CONV_PAD)
        next0 = pl.multiple_of(jnp.minimum(r0 + CHUNK, seq - CONV_PAD), CONV_PAD)
        pieces = []
        for src_ref in (x_ref, b_ref, c_ref):
            width = src_ref.shape[1]
            pieces.append(jnp.concatenate([
                jnp.where(c > 0, src_ref[pl.ds(prev0, CONV_PAD), :], zero16),
                src_ref[pl.ds(r0, CHUNK), :],
                jnp.where(c < nc - 1, src_ref[pl.ds(next0, CONV_PAD), :], zero16),
                jnp.zeros((CONV_K - CHUNK - 2 * CONV_PAD, width), BF16)], axis=0))
        stage = jnp.concatenate(pieces, axis=1)
        mid = SSM_CONV // 2
        acc = stage[CONV_PAD:CONV_PAD + CHUNK, :].astype(F32) * w_all[mid:mid + 1, :]
        for idx, t in enumerate(CONV_SIDE_TAPS):
            acc = acc + jnp.dot(sh_sc[idx], stage, preferred_element_type=F32) * w_all[t:t + 1, :]
        out = _silu(acc + b_all).astype(BF16)
        xs_sc[pl.ds(r0, CHUNK), :] = out[:, :gw]
        bm_sc[pl.ds(r0, CHUNK), :] = out[:, gw:gw + SSM_STATE]
        cm_sc[pl.ds(r0, CHUNK), :] = out[:, gw + SSM_STATE:]
        return carry

    lax.fori_loop(0, nc, conv_body, 0, unroll=4)

    def stage_back(c, slot_buf):
        blk = c // pack
        slot = c - blk * pack
        lhs = jnp.concatenate([wgt_sc[blk], din_sc[blk, 0:8, :]], axis=0)
        ex = expand(lhs, slot, gw, gw)
        xw_sc[slot_buf] = (xs_sc[chunk_rows(c), :].astype(F32) * ex[:CHUNK, :]).astype(BF16)
        dch_sc[slot_buf] = ex[CHUNK:CHUNK + 8, :]

    st_sc[...] = jnp.zeros_like(st_sc)
    stage_back(nc - 1, 0)

    def sweep_back(t, carry):
        c = nc - 1 - t
        buf = jnp.bitwise_and(t, 1)
        stb_sc[c] = st_sc[...].astype(BF16)
        upd = lax.dot_general(bm_sc[chunk_rows(c), :], xw_sc[buf], tdims,
                              preferred_element_type=F32)
        st_sc[...] = st_sc[...] * dch_sc[buf][0:1, :] + upd
        stage_back(jnp.maximum(c - 1, 0), 1 - buf)
        return carry

    lax.fori_loop(0, nc, sweep_back, 0, unroll=2)

    def mix(c):
        rows = chunk_rows(c)
        blk = c // pack
        slot = c - blk * pack
        unpack = jnp.bitwise_and(LANES - 2 * hpg * slot, LANES - 1)
        pcol = pltpu.roll(pcol_sc[blk], unpack, axis=1)
        prow_t = prowt_sc[blk, pl.ds(pl.multiple_of(2 * hpg * slot, 2 * hpg), 2 * hpg), :]
        xs = xs_sc[rows, :]
        bm = bm_sc[rows, :]
        cm = cm_sc[rows, :]
        cb = lax.dot_general(cm, bm, (((1,), (1,)), ((), ())), preferred_element_type=F32)
        cb_diag = jnp.sum(jnp.where(diag, cb, 0.0), axis=1, keepdims=True)
        ue = expand(jnp.where(fwd_lane, wgt_sc[blk], cb_diag * dtp_sc[blk]), slot, 0, 2 * gw)
        de = expand(din_sc[blk], slot, 0, 2 * gw)
        y_st = (jnp.dot(cm, st_sc[...].astype(BF16), preferred_element_type=F32) * de[:, :gw]
                + jnp.dot(cm, stb_sc[c], preferred_element_type=F32) * de[:, gw:])
        pieces = []
        for pr in range(hpg // 2):
            xs_p = xs[:, pr * LANES:(pr + 1) * LANES]
            m_pair = []
            for hh in (2 * pr, 2 * pr + 1):
                col_f = jnp.broadcast_to(pcol[:, hh:hh + 1], (CHUNK, CHUNK))
                col_b = jnp.broadcast_to(pcol[:, hpg + hh:hpg + hh + 1], (CHUNK, CHUNK))
                arg = jnp.where(causal, col_f - prow_t[hh:hh + 1, :],
                                col_b - prow_t[hpg + hh:hpg + hh + 1, :])
                m_pair.append((cb * jnp.exp2(arg)).astype(BF16))
            x_pair = jnp.concatenate([jnp.where(pair_lo, xs_p, zero16),
                                      jnp.where(pair_lo, zero16, xs_p)], axis=0)
            pieces.append(jnp.dot(jnp.concatenate(m_pair, axis=1), x_pair,
                                  preferred_element_type=F32))
        xs32 = xs.astype(F32)
        y_sc[...] = jnp.concatenate(pieces, axis=1) + y_st + (dskip_ref[...] + ue[:, gw:]) * xs32
        xw = (xs32 * ue[:, :gw]).astype(BF16)
        upd = lax.dot_general(bm, xw, tdims, preferred_element_type=F32)
        st_sc[...] = st_sc[...] * de[CHUNK - 1:CHUNK, :gw] + upd

    def finish(c):
        rows = chunk_rows(c)
        y = y_sc[...] * _silu(z_ref[rows, :].astype(F32))
        y = y * lax.rsqrt(jnp.mean(y * y, axis=-1, keepdims=True) + EPS)
        o_ref[rows, :] = (y * ng_ref[...]).astype(BF16)

    st_sc[...] = jnp.zeros_like(st_sc)
    mix(0)

    def sweep_fwd(c, carry):
        finish(c - 1)
        mix(c)
        return carry

    lax.fori_loop(1, nc, sweep_fwd, 0, unroll=4)
    finish(nc - 1)


def _ssd(proj, dt, cw_x, cw_b, cw_c, cb_x, cb_b, cb_c, bias_grp, alog_grp, dskip, norm_g, bsz, seq):
    nc = seq // CHUNK
    zb = COL_Z // SSM_GROUP_W
    xb = COL_X // SSM_GROUP_W
    bb = COL_B // SSM_STATE
    cb = COL_C // SSM_STATE
    gw = SSM_GROUP_W
    pack = min(LANES // PACK_LANES, nc)
    assert nc % pack == 0
    nblk = nc // pack
    packed = pltpu.VMEM((nblk, CHUNK, LANES), F32)
    return pl.pallas_call(
        functools.partial(_ssd_kernel, nc=nc, pack=pack),
        grid=(bsz, SSM_GROUPS),
        in_specs=[
            pl.BlockSpec((seq, gw), lambda b, g: (b, zb + g)),
            pl.BlockSpec((seq, gw), lambda b, g: (b, xb + g)),
            pl.BlockSpec((seq, SSM_STATE), lambda b, g: (b, bb + g)),
            pl.BlockSpec((seq, SSM_STATE), lambda b, g: (b, cb + g)),
            pl.BlockSpec((seq, LANES), lambda b, g: (b, g)),
            pl.BlockSpec((SSM_CONV, gw), lambda b, g: (0, g)),
            pl.BlockSpec((SSM_CONV, SSM_STATE), lambda b, g: (0, g)),
            pl.BlockSpec((SSM_CONV, SSM_STATE), lambda b, g: (0, g)),
            pl.BlockSpec((1, gw), lambda b, g: (0, g)),
            pl.BlockSpec((1, SSM_STATE), lambda b, g: (0, g)),
            pl.BlockSpec((1, SSM_STATE), lambda b, g: (0, g)),
            pl.BlockSpec((1, 1, LANES), lambda b, g: (g, 0, 0)),
            pl.BlockSpec((1, 1, LANES), lambda b, g: (g, 0, 0)),
            pl.BlockSpec((1, gw), lambda b, g: (0, g)),
            pl.BlockSpec((1, gw), lambda b, g: (0, g)),
        ],
        out_specs=pl.BlockSpec((seq, gw), lambda b, g: (b, g)),
        out_shape=jax.ShapeDtypeStruct((bsz * seq, SSM_INNER), BF16),
        scratch_shapes=[
            pltpu.VMEM((seq, gw), BF16),
            pltpu.VMEM((seq, SSM_STATE), BF16),
            pltpu.VMEM((seq, SSM_STATE), BF16),
            pltpu.VMEM((nc, SSM_STATE, gw), BF16),
            pltpu.VMEM((SSM_STATE, gw), F32),
            pltpu.VMEM((len(CONV_SIDE_TAPS), CHUNK, CONV_K), BF16),
            pltpu.VMEM((2 * LANES, 2 * gw), BF16),
            packed, packed, packed, packed, packed,
            pltpu.VMEM((CHUNK, gw), F32),
            pltpu.VMEM((2, CHUNK, gw), BF16),
            pltpu.VMEM((2, 8, gw), F32),
        ],
        compiler_params=_cparams(("parallel", "arbitrary")),
        name="ssd",
    )(proj, proj, proj, proj, dt, cw_x, cw_b, cw_c, cb_x, cb_b, cb_c, bias_grp, alog_grp,
      dskip, norm_g)


def _merge_kernel(yr_ref, g_ref, gn_ref, ys_ref, gr_ref, gs_ref, x_ref, wr_ref, ws_ref, wo_ref,
                  o_ref):
    y_ret = None
    for hd in range(RET_HEADS):
        cols = slice(hd * RET_V_DIM, (hd + 1) * RET_V_DIM)
        y = yr_ref[:, cols].astype(F32)
        mu = jnp.mean(y, axis=-1, keepdims=True)
        d = y - mu
        var = jnp.mean(d * d, axis=-1, keepdims=True)
        yn = (d * lax.rsqrt(var + EPS)) * gn_ref[:, cols]
        act = (yn * _silu(g_ref[:, cols].astype(F32))).astype(BF16)
        part = jnp.dot(act, wr_ref[cols, :], preferred_element_type=F32)
        y_ret = part if y_ret is None else y_ret + part
    y_ssm = jnp.dot(ys_ref[...], ws_ref[...], preferred_element_type=F32)
    mixed = (_sigmoid(gr_ref[...].astype(F32)) * y_ret
             + _sigmoid(gs_ref[...].astype(F32)) * y_ssm)
    o_ref[...] = x_ref[...] + jnp.dot(mixed.astype(BF16), wo_ref[...], preferred_element_type=F32)


def _resident(shape):
    return pl.BlockSpec(shape, lambda i: (0,) * len(shape), pipeline_mode=pl.Buffered(1))


def _merge(yret, yssm, proj, x2, gn_g, w_ret_o, w_ssm_o, w_out, tm):
    m = x2.shape[0]
    gb = COL_G // RET_V
    grb = COL_GATE_RET // D_MODEL
    gsb = COL_GATE_SSM // D_MODEL
    return pl.pallas_call(
        _merge_kernel,
        grid=(m // tm,),
        in_specs=[
            pl.BlockSpec((tm, RET_V), lambda i: (i, 0)),
            pl.BlockSpec((tm, RET_V), lambda i: (i, gb)),
            _resident((1, RET_V)),
            pl.BlockSpec((tm, SSM_INNER), lambda i: (i, 0)),
            pl.BlockSpec((tm, D_MODEL), lambda i: (i, grb)),
            pl.BlockSpec((tm, D_MODEL), lambda i: (i, gsb)),
            pl.BlockSpec((tm, D_MODEL), lambda i: (i, 0)),
            _resident((RET_V, D_MODEL)),
            _resident((SSM_INNER, D_MODEL)),
            _resident((D_MODEL, D_MODEL)),
        ],
        out_specs=pl.BlockSpec((tm, D_MODEL), lambda i: (i, 0)),
        out_shape=jax.ShapeDtypeStruct((m, D_MODEL), F32),
        compiler_params=_cparams(("parallel",)),
        name="merge",
    )(yret, proj, gn_g, yssm, proj, proj, x2, w_ret_o, w_ssm_o, w_out)


def _rms(x, g):
    return (x * lax.rsqrt(jnp.mean(x * x, axis=-1, keepdims=True) + EPS)) * g


def _mlp_kernel(x_ref, gm_ref, wu_ref, wd_ref, gf_ref, o_ref):
    x = x_ref[...]
    h = _rms(x, gm_ref[...]).astype(BF16)
    up = jnp.dot(h, wu_ref[...], preferred_element_type=F32)
    act = jnp.square(jnp.maximum(up, 0.0)).astype(BF16)
    x2 = x + jnp.dot(act, wd_ref[...], preferred_element_type=F32)
    o_ref[...] = _rms(x2, gf_ref[...])


def _mlp(x1, g_mlp, w_up, w_down, g_final, tm):
    m = x1.shape[0]
    return pl.pallas_call(
        _mlp_kernel,
        grid=(m // tm,),
        in_specs=[
            pl.BlockSpec((tm, D_MODEL), lambda i: (i, 0)),
            _resident((1, D_MODEL)),
            _resident((D_MODEL, D_FF)),
            _resident((D_FF, D_MODEL)),
            _resident((1, D_MODEL)),
        ],
        out_specs=pl.BlockSpec((tm, D_MODEL), lambda i: (i, 0)),
        out_shape=jax.ShapeDtypeStruct((m, D_MODEL), F32),
        compiler_params=_cparams(("parallel",)),
        name="mlp",
    )(x1, g_mlp, w_up, w_down, g_final)


def _group_rows(fwd, bwd):
    f = fwd.reshape(SSM_GROUPS, SSM_HPG)
    b = bwd.reshape(SSM_GROUPS, SSM_HPG)
    one = jnp.concatenate([f, b], axis=1)
    return jnp.tile(one, (1, LANES // PACK_LANES)).reshape(SSM_GROUPS, 1, LANES)


def _layer(x2, pos2, bsz, seq, norm_mix_g, w_in, ret_gn_g, w_ret_o, conv_w, conv_b, dt_bias_f,
           dt_bias_b, a_log_f, a_log_b, ssm_d, ssm_norm_g, w_ssm_o, w_out, norm_mlp_g, w_mlp_up,
           w_mlp_down, g_final):
    m = bsz * seq
    dt0 = COL_C + SSM_GROUPS * SSM_STATE
    w_main = jnp.concatenate([w_in[:, :dt0], w_in[:, dt0 + 2 * SSM_HEADS:]], axis=1).astype(BF16)
    wdt = w_in[:, dt0:dt0 + 2 * SSM_HEADS]
    wdt_f = wdt[:, :SSM_HEADS].reshape(D_MODEL, SSM_GROUPS, SSM_HPG)
    wdt_b = wdt[:, SSM_HEADS:].reshape(D_MODEL, SSM_GROUPS, SSM_HPG)
    wdt_pad = jnp.zeros((D_MODEL, SSM_GROUPS, LANES - 2 * SSM_HPG), F32)
    w_dt = jnp.concatenate([wdt_f, wdt_b, wdt_pad], axis=2).reshape(D_MODEL, SSM_GROUPS * LANES)
    w_dt = w_dt.astype(BF16)

    inv = (ROPE_BASE ** (-jnp.arange(ROPE_HALF, dtype=F32) / ROPE_HALF)).reshape(1, ROPE_HALF)
    tm1 = min(1024, m)
    proj, dt = _inproj(x2, pos2, inv, norm_mix_g.reshape(1, D_MODEL), w_main, w_dt, tm1)

    logg = jnp.log1p(-jnp.exp2(-5.0 - jnp.arange(RET_HEADS, dtype=F32)))
    yret = _retention(proj, logg, bsz, seq)

    nbc = SSM_GROUPS * SSM_STATE
    cw_x, cw_b, cw_c = (conv_w[:, :SSM_INNER], conv_w[:, SSM_INNER:SSM_INNER + nbc],
                        conv_w[:, SSM_INNER + nbc:])
    cb2 = conv_b.reshape(1, -1)
    cb_x, cb_b, cb_c = (cb2[:, :SSM_INNER], cb2[:, SSM_INNER:SSM_INNER + nbc],
                        cb2[:, SSM_INNER + nbc:])
    dskip = jnp.repeat(ssm_d, SSM_HEAD_DIM).reshape(1, SSM_INNER)
    yssm = _ssd(proj, dt, cw_x, cw_b, cw_c, cb_x, cb_b, cb_c,
                _group_rows(dt_bias_f, dt_bias_b), _group_rows(a_log_f, a_log_b),
                dskip, ssm_norm_g.reshape(1, SSM_INNER), bsz, seq)

    tm4 = min(512, m)
    x1 = _merge(yret, yssm, proj, x2, ret_gn_g.reshape(1, RET_V), w_ret_o.astype(BF16),
                w_ssm_o.astype(BF16), w_out.astype(BF16), tm4)
    return _mlp(x1, norm_mlp_g.reshape(1, D_MODEL), w_mlp_up.astype(BF16),
                w_mlp_down.astype(BF16), g_final, tm4)


def kernel(x, positions, norm_mix_g, w_in, ret_gn_g, w_ret_o, conv_w, conv_b, dt_bias_f, dt_bias_b,
           a_log_f, a_log_b, ssm_d, ssm_norm_g, w_ssm_o, w_out, norm_mlp_g, w_mlp_up, w_mlp_down,
           norm_final_g):
    bsz, seq, _ = x.shape
    depth = w_in.shape[0]
    assert depth == 1, "the final-norm fusion in the MLP call assumes a single layer"
    x2 = x.reshape(bsz * seq, D_MODEL)
    pos2 = jnp.broadcast_to(positions.reshape(bsz * seq, 1).astype(F32), (bsz * seq, ROPE_HALF))
    out = _layer(x2, pos2, bsz, seq, norm_mix_g[0], w_in[0], ret_gn_g[0], w_ret_o[0], conv_w[0],
                 conv_b[0], dt_bias_f[0], dt_bias_b[0], a_log_f[0], a_log_b[0], ssm_d[0],
                 ssm_norm_g[0], w_ssm_o[0], w_out[0], norm_mlp_g[0], w_mlp_up[0], w_mlp_down[0],
                 norm_final_g.reshape(1, D_MODEL))
    return out.reshape(bsz, seq, D_MODEL)
```

```python
import functools

import jax
import jax.numpy as jnp
from jax import lax
from jax.experimental import pallas as pl
from jax.experimental.pallas import tpu as pltpu

F32 = jnp.float32
BF16 = jnp.bfloat16

D_MODEL = 1024
RET_HEADS = 4
RET_QK_DIM = 256
RET_V_DIM = 512
RET_QK = RET_HEADS * RET_QK_DIM
RET_V = RET_HEADS * RET_V_DIM
ROPE_BASE = 10000.0
ROPE_HALF = RET_QK_DIM // 2
SSM_INNER = 2 * D_MODEL
SSM_HEAD_DIM = 64
SSM_HEADS = SSM_INNER // SSM_HEAD_DIM
SSM_GROUPS = 4
SSM_HPG = SSM_HEADS // SSM_GROUPS
SSM_STATE = 128
SSM_CONV = 5
SSM_GROUP_W = SSM_HPG * SSM_HEAD_DIM
D_FF = 4 * D_MODEL
EPS = 1e-6
CHUNK = 128
RET_CHUNK = 256

LANES = 128
BF16_ROWS = 16
MXU_K = 256
CONV_PAD = BF16_ROWS
CONV_K = MXU_K
CONV_SIDE_TAPS = tuple(t for t in range(SSM_CONV) if t != SSM_CONV // 2)
PACK_LANES = 2 * SSM_HPG
E_ORIGIN = LANES - PACK_LANES
LOG2E = 1.4426950408889634

COL_Q = 0
COL_K = COL_Q + RET_QK
COL_V = COL_K + RET_QK
COL_G = COL_V + RET_V
COL_Z = COL_G + RET_V
COL_X = COL_Z + SSM_INNER
COL_B = COL_X + SSM_INNER
COL_C = COL_B + SSM_GROUPS * SSM_STATE
COL_GATE_RET = COL_C + SSM_GROUPS * SSM_STATE
COL_GATE_SSM = COL_GATE_RET + D_MODEL
PROJ_W = COL_GATE_SSM + D_MODEL
INPROJ_TN = 2 * RET_QK

VMEM_LIMIT = 56 * 1024 * 1024


def _cparams(sem):
    return pltpu.CompilerParams(dimension_semantics=sem, vmem_limit_bytes=VMEM_LIMIT)


def _sigmoid(x):
    return 0.5 * jnp.tanh(0.5 * x) + 0.5


def _silu(x):
    return x * _sigmoid(x)


def _rope_store(acc, c, s, scale, proj_ref, col0):
    for hd in range(RET_HEADS):
        lo = hd * RET_QK_DIM
        t1 = acc[:, lo:lo + ROPE_HALF]
        t2 = acc[:, lo + ROPE_HALF:lo + RET_QK_DIM]
        o = col0 + lo
        proj_ref[:, o:o + ROPE_HALF] = ((t1 * c - t2 * s) * scale).astype(BF16)
        proj_ref[:, o + ROPE_HALF:o + RET_QK_DIM] = ((t1 * s + t2 * c) * scale).astype(BF16)


TAIL_STEP = 1


def _inproj_col_tile(j, n_full):
    return jnp.where(j == 0, 0, jnp.where(j == TAIL_STEP, n_full, j - 1))


def _inproj_kernel(x_ref, pos_ref, inv_ref, g_ref, w_ref, wdt_ref, proj_ref, dt_ref, h_sc, *, tail):
    j = pl.program_id(1)

    @pl.when(j == 0)
    def _():
        x = x_ref[...]
        ms = jnp.mean(x * x, axis=-1, keepdims=True)
        h = ((x * lax.rsqrt(ms + EPS)) * g_ref[...]).astype(BF16)
        h_sc[...] = h
        dt_ref[...] = jnp.dot(h, wdt_ref[...], preferred_element_type=F32)
        ang = pos_ref[...] * inv_ref[...]
        c = jnp.cos(ang)
        s = jnp.sin(ang)
        q = jnp.dot(h, w_ref[:, :RET_QK], preferred_element_type=F32)
        _rope_store(q, c, s, 1.0, proj_ref, COL_Q)
        k = jnp.dot(h, w_ref[:, RET_QK:2 * RET_QK], preferred_element_type=F32)
        _rope_store(k, c, s, RET_QK_DIM ** -0.5, proj_ref, COL_K)

    @pl.when(j > TAIL_STEP)
    def _():
        proj_ref[...] = jnp.dot(h_sc[...], w_ref[...], preferred_element_type=F32).astype(BF16)

    @pl.when(j == TAIL_STEP)
    def _():
        proj_ref[:, :tail] = jnp.dot(h_sc[...], w_ref[:, :tail],
                                     preferred_element_type=F32).astype(BF16)


def _inproj(x2, pos2, inv, g, w_main, w_dt, tm):
    m = x2.shape[0]
    tn = INPROJ_TN
    n_full, tail = divmod(PROJ_W, tn)
    assert COL_K + RET_QK == tn and tail > 0 and tail % LANES == 0 and n_full > TAIL_STEP
    col = functools.partial(_inproj_col_tile, n_full=n_full)
    return pl.pallas_call(
        functools.partial(_inproj_kernel, tail=tail),
        grid=(m // tm, n_full + 1),
        in_specs=[
            pl.BlockSpec((tm, D_MODEL), lambda i, j: (i, 0)),
            pl.BlockSpec((tm, ROPE_HALF), lambda i, j: (i, 0)),
            pl.BlockSpec((1, ROPE_HALF), lambda i, j: (0, 0)),
            pl.BlockSpec((1, D_MODEL), lambda i, j: (0, 0)),
            pl.BlockSpec((D_MODEL, tn), lambda i, j: (0, col(j))),
            pl.BlockSpec((D_MODEL, SSM_GROUPS * LANES), lambda i, j: (0, 0)),
        ],
        out_specs=[
            pl.BlockSpec((tm, tn), lambda i, j: (i, col(j))),
            pl.BlockSpec((tm, SSM_GROUPS * LANES), lambda i, j: (i, 0)),
        ],
        out_shape=[
            jax.ShapeDtypeStruct((m, PROJ_W), BF16),
            jax.ShapeDtypeStruct((m, SSM_GROUPS * LANES), F32),
        ],
        scratch_shapes=[
            pltpu.VMEM((tm, D_MODEL), BF16),
        ],
        compiler_params=_cparams(("parallel", "arbitrary")),
        name="inproj",
    )(x2, pos2, inv, g, w_main, w_dt)


def _ret_kernel(logg_ref, q_ref, k_ref, v_ref, o_ref, sb_sc, st_sc, *, nc):
    lg = logg_ref[pl.program_id(1)]
    pos = lax.broadcasted_iota(jnp.int32, (RET_CHUNK, RET_QK_DIM), 0).astype(F32)
    qdec_f = jnp.exp((pos + 1.0) * lg)
    kdec_f = jnp.exp((RET_CHUNK - 1.0 - pos) * lg)
    qdec_b = jnp.exp((RET_CHUNK - pos) * lg)
    kdec_b = jnp.exp(pos * lg)
    chunk_dec = jnp.exp(jnp.full((1, RET_V_DIM), float(RET_CHUNK), F32) * lg)
    ii = lax.broadcasted_iota(jnp.int32, (RET_CHUNK, RET_CHUNK), 0)
    jj = lax.broadcasted_iota(jnp.int32, (RET_CHUNK, RET_CHUNK), 1)
    intra_dec = jnp.exp(jnp.abs(ii - jj).astype(F32) * lg)
    tdims = (((0,), (0,)), ((), ()))

    st_sc[...] = jnp.zeros_like(st_sc)

    def sweep_back(t, carry):
        c = nc - 1 - t
        rows = pl.ds(pl.multiple_of(c * RET_CHUNK, RET_CHUNK), RET_CHUNK)
        sb_sc[c] = st_sc[...].astype(BF16)
        kb = (k_ref[rows, :].astype(F32) * kdec_b).astype(BF16)
        upd = lax.dot_general(kb, v_ref[rows, :], tdims, preferred_element_type=F32)
        st_sc[...] = st_sc[...] * chunk_dec + upd
        return carry

    lax.fori_loop(0, nc, sweep_back, 0, unroll=4)

    st_sc[...] = jnp.zeros_like(st_sc)

    def sweep_fwd(c, carry):
        rows = pl.ds(pl.multiple_of(c * RET_CHUNK, RET_CHUNK), RET_CHUNK)
        qb16 = q_ref[rows, :]
        kb16 = k_ref[rows, :]
        vc = v_ref[rows, :]
        qf32 = qb16.astype(F32)
        scores = lax.dot_general(qb16, kb16, (((1,), (1,)), ((), ())), preferred_element_type=F32)
        y = jnp.dot((scores * intra_dec).astype(BF16), vc, preferred_element_type=F32)
        y = y + jnp.dot((qf32 * qdec_f).astype(BF16), st_sc[...].astype(BF16),
                        preferred_element_type=F32)
        y = y + jnp.dot((qf32 * qdec_b).astype(BF16), sb_sc[c], preferred_element_type=F32)
        o_ref[rows, :] = y.astype(BF16)
        kf = (kb16.astype(F32) * kdec_f).astype(BF16)
        upd = lax.dot_general(kf, vc, tdims, preferred_element_type=F32)
        st_sc[...] = st_sc[...] * chunk_dec + upd
        return carry

    lax.fori_loop(0, nc, sweep_fwd, 0, unroll=4)


def _retention(proj, logg, bsz, seq):
    nc = seq // RET_CHUNK
    qb = COL_Q // RET_QK_DIM
    kb = COL_K // RET_QK_DIM
    vb = COL_V // RET_V_DIM
    return pl.pallas_call(
        functools.partial(_ret_kernel, nc=nc),
        grid=(bsz, RET_HEADS),
        in_specs=[
            pl.BlockSpec(memory_space=pltpu.SMEM),
            pl.BlockSpec((seq, RET_QK_DIM), lambda b, h: (b, qb + h)),
            pl.BlockSpec((seq, RET_QK_DIM), lambda b, h: (b, kb + h)),
            pl.BlockSpec((seq, RET_V_DIM), lambda b, h: (b, vb + h)),
        ],
        out_specs=pl.BlockSpec((seq, RET_V_DIM), lambda b, h: (b, h)),
        out_shape=jax.ShapeDtypeStruct((bsz * seq, RET_V), BF16),
        scratch_shapes=[
            pltpu.VMEM((nc, RET_QK_DIM, RET_V_DIM), BF16),
            pltpu.VMEM((RET_QK_DIM, RET_V_DIM), F32),
        ],
        compiler_params=_cparams(("parallel", "arbitrary")),
        name="retention",
    )(logg, proj, proj, proj)


def _ssd_kernel(z_ref, x_ref, b_ref, c_ref, dt_ref, cwx_ref, cwb_ref, cwc_ref,
                cbx_ref, cbb_ref, cbc_ref, bias_ref, alog_ref, dskip_ref, ng_ref,
                o_ref, xs_sc, bm_sc, cm_sc, stb_sc, st_sc, sh_sc, e_sc,
                pcol_sc, prowt_sc, din_sc, wgt_sc, dtp_sc, y_sc, xw_sc, dch_sc, *, nc, pack):
    seq = nc * CHUNK
    hpg = SSM_HPG
    gw = SSM_GROUP_W
    cw = gw + 2 * SSM_STATE
    nblk = nc // pack
    tdims = (((0,), (0,)), ((), ()))

    def chunk_rows(c):
        return pl.ds(pl.multiple_of(c * CHUNK, CHUNK), CHUNK)

    si = lax.broadcasted_iota(jnp.int32, (CHUNK, CONV_K), 0)
    sr = lax.broadcasted_iota(jnp.int32, (CHUNK, CONV_K), 1)
    for idx, t in enumerate(CONV_SIDE_TAPS):
        sh_sc[idx] = jnp.where(sr == si + CONV_PAD + t - SSM_CONV // 2, 1.0, 0.0).astype(BF16)
    er = lax.broadcasted_iota(jnp.int32, (2 * LANES, 2 * gw), 0)
    ec = lax.broadcasted_iota(jnp.int32, (2 * LANES, 2 * gw), 1) // SSM_HEAD_DIM
    e_sc[...] = jnp.where(er - E_ORIGIN == ec, 1.0, 0.0).astype(BF16)

    def expand(v, slot, col0, ncol):
        e = e_sc[pl.ds(pl.multiple_of(E_ORIGIN - 2 * hpg * slot, 2 * hpg), LANES),
                 col0:col0 + ncol]
        return jnp.dot(v.astype(BF16), e, preferred_element_type=F32)

    lane = lax.broadcasted_iota(jnp.int32, (CHUNK, LANES), 1)
    fwd_lane = jnp.bitwise_and(lane, 2 * hpg - 1) < hpg
    first_slot = lane < 2 * hpg
    ii = lax.broadcasted_iota(jnp.int32, (CHUNK, CHUNK), 0)
    jj = lax.broadcasted_iota(jnp.int32, (CHUNK, CHUNK), 1)
    tril = jnp.where(jj <= ii, 1.0, 0.0).astype(F32)
    causal = jj <= ii
    diag = ii == jj
    pair_lo = lane < SSM_HEAD_DIM
    bias_row = bias_ref[0]
    a_row = -jnp.exp(alog_ref[0])

    def pre_body(blk, carry):
        raw = jnp.zeros((CHUNK, LANES), F32)
        for k in range(pack):
            t = jnp.where(first_slot, dt_ref[chunk_rows(blk * pack + k), :], 0.0)
            raw = raw + (pltpu.roll(t, 2 * hpg * k, axis=1) if k else t)
        dtp = jax.nn.softplus(raw + bias_row)
        la = dtp * a_row
        cum = jnp.dot(tril, la, preferred_element_type=F32, precision=lax.Precision.HIGHEST)
        tot = cum[CHUNK - 1:CHUNK, :]
        excl = cum - la
        pcol = jnp.where(fwd_lane, cum, tot - excl)
        pcol_sc[blk] = pcol * LOG2E
        prowt_sc[blk] = ((pcol - jnp.log(dtp)) * LOG2E).T
        din_sc[blk] = jnp.exp(pcol)
        wgt_sc[blk] = jnp.exp(jnp.where(fwd_lane, tot - cum, excl)) * dtp
        dtp_sc[blk] = dtp
        return carry

    lax.fori_loop(0, nblk, pre_body, 0, unroll=True)

    w_all = jnp.concatenate([cwx_ref[...], cwb_ref[...], cwc_ref[...]], axis=1)
    b_all = jnp.concatenate([cbx_ref[...], cbb_ref[...], cbc_ref[...]], axis=1)
    zero16 = jnp.zeros((), BF16)

    def conv_body(c, carry):
        r0 = pl.multiple_of(c * CHUNK, CHUNK)
        prev0 = pl.multiple_of(jnp.maximum(r0 - CONV_PAD, 0), CONV_PAD)
        next0 = pl.multiple_of(jnp.minimum(r0 + CHUNK, seq - CONV_PAD), CONV_PAD)
        pieces = []
        for src_ref in (x_ref, b_ref, c_ref):
            width = src_ref.shape[1]
            pieces.append(jnp.concatenate([
                jnp.where(c > 0, src_ref[pl.ds(prev0, CONV_PAD), :], zero16),
                src_ref[pl.ds(r0, CHUNK), :],
                jnp.where(c < nc - 1, src_ref[pl.ds(next0, CONV_PAD), :], zero16),
                jnp.zeros((CONV_K - CHUNK - 2 * CONV_PAD, width), BF16)], axis=0))
        stage = jnp.concatenate(pieces, axis=1)
        mid = SSM_CONV // 2
        acc = stage[CONV_PAD:CONV_PAD + CHUNK, :].astype(F32) * w_all[mid:mid + 1, :]
        for idx, t in enumerate(CONV_SIDE_TAPS):
            acc = acc + jnp.dot(sh_sc[idx], stage, preferred_element_type=F32) * w_all[t:t + 1, :]
        out = _silu(acc + b_all).astype(BF16)
        xs_sc[pl.ds(r0, CHUNK), :] = out[:, :gw]
        bm_sc[pl.ds(r0, CHUNK), :] = out[:, gw:gw + SSM_STATE]
        cm_sc[pl.ds(r0, CHUNK), :] = out[:, gw + SSM_STATE:]
        return carry

    lax.fori_loop(0, nc, conv_body, 0, unroll=4)

    def stage_back(c, slot_buf):
        blk = c // pack
        slot = c - blk * pack
        lhs = jnp.concatenate([wgt_sc[blk], din_sc[blk, 0:8, :]], axis=0)
        ex = expand(lhs, slot, gw, gw)
        xw_sc[slot_buf] = (xs_sc[chunk_rows(c), :].astype(F32) * ex[:CHUNK, :]).astype(BF16)
        dch_sc[slot_buf] = ex[CHUNK:CHUNK + 8, :]

    st_sc[...] = jnp.zeros_like(st_sc)
    stage_back(nc - 1, 0)

    def sweep_back(t, carry):
        c = nc - 1 - t
        buf = jnp.bitwise_and(t, 1)
        stb_sc[c] = st_sc[...].astype(BF16)
        upd = lax.dot_general(bm_sc[chunk_rows(c), :], xw_sc[buf], tdims,
                              preferred_element_type=F32)
        st_sc[...] = st_sc[...] * dch_sc[buf][0:1, :] + upd
        stage_back(jnp.maximum(c - 1, 0), 1 - buf)
        return carry

    lax.fori_loop(0, nc, sweep_back, 0, unroll=2)

    def mix(c):
        rows = chunk_rows(c)
        blk = c // pack
        slot = c - blk * pack
        unpack = jnp.bitwise_and(LANES - 2 * hpg * slot, LANES - 1)
        pcol = pltpu.roll(pcol_sc[blk], unpack, axis=1)
        prow_t = prowt_sc[blk, pl.ds(pl.multiple_of(2 * hpg * slot, 2 * hpg), 2 * hpg), :]
        xs = xs_sc[rows, :]
        bm = bm_sc[rows, :]
        cm = cm_sc[rows, :]
        cb = lax.dot_general(cm, bm, (((1,), (1,)), ((), ())), preferred_element_type=F32)
        cb_diag = jnp.sum(jnp.where(diag, cb, 0.0), axis=1, keepdims=True)
        ue = expand(jnp.where(fwd_lane, wgt_sc[blk], cb_diag * dtp_sc[blk]), slot, 0, 2 * gw)
        de = expand(din_sc[blk], slot, 0, 2 * gw)
        y_st = (jnp.dot(cm, st_sc[...].astype(BF16), preferred_element_type=F32) * de[:, :gw]
                + jnp.dot(cm, stb_sc[c], preferred_element_type=F32) * de[:, gw:])
        pieces = []
        for pr in range(hpg // 2):
            xs_p = xs[:, pr * LANES:(pr + 1) * LANES]
            m_pair = []
            for hh in (2 * pr, 2 * pr + 1):
                col_f = jnp.broadcast_to(pcol[:, hh:hh + 1], (CHUNK, CHUNK))
                col_b = jnp.broadcast_to(pcol[:, hpg + hh:hpg + hh + 1], (CHUNK, CHUNK))
                arg = jnp.where(causal, col_f - prow_t[hh:hh + 1, :],
                                col_b - prow_t[hpg + hh:hpg + hh + 1, :])
                m_pair.append((cb * jnp.exp2(arg)).astype(BF16))
            x_pair = jnp.concatenate([jnp.where(pair_lo, xs_p, zero16),
                                      jnp.where(pair_lo, zero16, xs_p)], axis=0)
            pieces.append(jnp.dot(jnp.concatenate(m_pair, axis=1), x_pair,
                                  preferred_element_type=F32))
        xs32 = xs.astype(F32)
        y_sc[...] = jnp.concatenate(pieces, axis=1) + y_st + (dskip_ref[...] + ue[:, gw:]) * xs32
        xw = (xs32 * ue[:, :gw]).astype(BF16)
        upd = lax.dot_general(bm, xw, tdims, preferred_element_type=F32)
        st_sc[...] = st_sc[...] * de[CHUNK - 1:CHUNK, :gw] + upd

    def finish(c):
        rows = chunk_rows(c)
        y = y_sc[...] * _silu(z_ref[rows, :].astype(F32))
        y = y * lax.rsqrt(jnp.mean(y * y, axis=-1, keepdims=True) + EPS)
        o_ref[rows, :] = (y * ng_ref[...]).astype(BF16)

    st_sc[...] = jnp.zeros_like(st_sc)
    mix(0)

    def sweep_fwd(c, carry):
        finish(c - 1)
        mix(c)
        return carry

    lax.fori_loop(1, nc, sweep_fwd, 0, unroll=4)
    finish(nc - 1)


def _ssd(proj, dt, cw_x, cw_b, cw_c, cb_x, cb_b, cb_c, bias_grp, alog_grp, dskip, norm_g, bsz, seq):
    nc = seq // CHUNK
    zb = COL_Z // SSM_GROUP_W
    xb = COL_X // SSM_GROUP_W
    bb = COL_B // SSM_STATE
    cb = COL_C // SSM_STATE
    gw = SSM_GROUP_W
    pack = min(LANES // PACK_LANES, nc)
    assert nc % pack == 0
    nblk = nc // pack
    packed = pltpu.VMEM((nblk, CHUNK, LANES), F32)
    return pl.pallas_call(
        functools.partial(_ssd_kernel, nc=nc, pack=pack),
        grid=(bsz, SSM_GROUPS),
        in_specs=[
            pl.BlockSpec((seq, gw), lambda b, g: (b, zb + g)),
            pl.BlockSpec((seq, gw), lambda b, g: (b, xb + g)),
            pl.BlockSpec((seq, SSM_STATE), lambda b, g: (b, bb + g)),
            pl.BlockSpec((seq, SSM_STATE), lambda b, g: (b, cb + g)),
            pl.BlockSpec((seq, LANES), lambda b, g: (b, g)),
            pl.BlockSpec((SSM_CONV, gw), lambda b, g: (0, g)),
            pl.BlockSpec((SSM_CONV, SSM_STATE), lambda b, g: (0, g)),
            pl.BlockSpec((SSM_CONV, SSM_STATE), lambda b, g: (0, g)),
            pl.BlockSpec((1, gw), lambda b, g: (0, g)),
            pl.BlockSpec((1, SSM_STATE), lambda b, g: (0, g)),
            pl.BlockSpec((1, SSM_STATE), lambda b, g: (0, g)),
            pl.BlockSpec((1, 1, LANES), lambda b, g: (g, 0, 0)),
            pl.BlockSpec((1, 1, LANES), lambda b, g: (g, 0, 0)),
            pl.BlockSpec((1, gw), lambda b, g: (0, g)),
            pl.BlockSpec((1, gw), lambda b, g: (0, g)),
        ],
        out_specs=pl.BlockSpec((seq, gw), lambda b, g: (b, g)),
        out_shape=jax.ShapeDtypeStruct((bsz * seq, SSM_INNER), BF16),
        scratch_shapes=[
            pltpu.VMEM((seq, gw), BF16),
            pltpu.VMEM((seq, SSM_STATE), BF16),
            pltpu.VMEM((seq, SSM_STATE), BF16),
            pltpu.VMEM((nc, SSM_STATE, gw), BF16),
            pltpu.VMEM((SSM_STATE, gw), F32),
            pltpu.VMEM((len(CONV_SIDE_TAPS), CHUNK, CONV_K), BF16),
            pltpu.VMEM((2 * LANES, 2 * gw), BF16),
            packed, packed, packed, packed, packed,
            pltpu.VMEM((CHUNK, gw), F32),
            pltpu.VMEM((2, CHUNK, gw), BF16),
            pltpu.VMEM((2, 8, gw), F32),
        ],
        compiler_params=_cparams(("parallel", "arbitrary")),
        name="ssd",
    )(proj, proj, proj, proj, dt, cw_x, cw_b, cw_c, cb_x, cb_b, cb_c, bias_grp, alog_grp,
      dskip, norm_g)


def _merge_kernel(yr_ref, g_ref, gn_ref, ys_ref, gr_ref, gs_ref, x_ref, wr_ref, ws_ref, wo_ref,
                  o_ref):
    y_ret = None
    for hd in range(RET_HEADS):
        cols = slice(hd * RET_V_DIM, (hd + 1) * RET_V_DIM)
        y = yr_ref[:, cols].astype(F32)
        mu = jnp.mean(y, axis=-1, keepdims=True)
        d = y - mu
        var = jnp.mean(d * d, axis=-1, keepdims=True)
        yn = (d * lax.rsqrt(var + EPS)) * gn_ref[:, cols]
        act = (yn * _silu(g_ref[:, cols].astype(F32))).astype(BF16)
        part = jnp.dot(act, wr_ref[cols, :], preferred_element_type=F32)
        y_ret = part if y_ret is None else y_ret + part
    y_ssm = jnp.dot(ys_ref[...], ws_ref[...], preferred_element_type=F32)
    mixed = (_sigmoid(gr_ref[...].astype(F32)) * y_ret
             + _sigmoid(gs_ref[...].astype(F32)) * y_ssm)
    o_ref[...] = x_ref[...] + jnp.dot(mixed.astype(BF16), wo_ref[...], preferred_element_type=F32)


def _resident(shape):
    return pl.BlockSpec(shape, lambda i: (0,) * len(shape), pipeline_mode=pl.Buffered(1))


def _merge(yret, yssm, proj, x2, gn_g, w_ret_o, w_ssm_o, w_out, tm):
    m = x2.shape[0]
    gb = COL_G // RET_V
    grb = COL_GATE_RET // D_MODEL
    gsb = COL_GATE_SSM // D_MODEL
    return pl.pallas_call(
        _merge_kernel,
        grid=(m // tm,),
        in_specs=[
            pl.BlockSpec((tm, RET_V), lambda i: (i, 0)),
            pl.BlockSpec((tm, RET_V), lambda i: (i, gb)),
            _resident((1, RET_V)),
            pl.BlockSpec((tm, SSM_INNER), lambda i: (i, 0)),
            pl.BlockSpec((tm, D_MODEL), lambda i: (i, grb)),
            pl.BlockSpec((tm, D_MODEL), lambda i: (i, gsb)),
            pl.BlockSpec((tm, D_MODEL), lambda i: (i, 0)),
            _resident((RET_V, D_MODEL)),
            _resident((SSM_INNER, D_MODEL)),
            _resident((D_MODEL, D_MODEL)),
        ],
        out_specs=pl.BlockSpec((tm, D_MODEL), lambda i: (i, 0)),
        out_shape=jax.ShapeDtypeStruct((m, D_MODEL), F32),
        compiler_params=_cparams(("parallel",)),
        name="merge",
    )(yret, proj, gn_g, yssm, proj, proj, x2, w_ret_o, w_ssm_o, w_out)


def _rms(x, g):
    return (x * lax.rsqrt(jnp.mean(x * x, axis=-1, keepdims=True) + EPS)) * g


def _mlp_kernel(x_ref, gm_ref, wu_ref, wd_ref, gf_ref, o_ref):
    x = x_ref[...]
    h = _rms(x, gm_ref[...]).astype(BF16)
    up = jnp.dot(h, wu_ref[...], preferred_element_type=F32)
    act = jnp.square(jnp.maximum(up, 0.0)).astype(BF16)
    x2 = x + jnp.dot(act, wd_ref[...], preferred_element_type=F32)
    o_ref[...] = _rms(x2, gf_ref[...])


def _mlp(x1, g_mlp, w_up, w_down, g_final, tm):
    m = x1.shape[0]
    return pl.pallas_call(
        _mlp_kernel,
        grid=(m // tm,),
        in_specs=[
            pl.BlockSpec((tm, D_MODEL), lambda i: (i, 0)),
            _resident((1, D_MODEL)),
            _resident((D_MODEL, D_FF)),
            _resident((D_FF, D_MODEL)),
            _resident((1, D_MODEL)),
        ],
        out_specs=pl.BlockSpec((tm, D_MODEL), lambda i: (i, 0)),
        out_shape=jax.ShapeDtypeStruct((m, D_MODEL), F32),
        compiler_params=_cparams(("parallel",)),
        name="mlp",
    )(x1, g_mlp, w_up, w_down, g_final)


def _group_rows(fwd, bwd):
    f = fwd.reshape(SSM_GROUPS, SSM_HPG)
    b = bwd.reshape(SSM_GROUPS, SSM_HPG)
    one = jnp.concatenate([f, b], axis=1)
    return jnp.tile(one, (1, LANES // PACK_LANES)).reshape(SSM_GROUPS, 1, LANES)


def _layer(x2, pos2, bsz, seq, norm_mix_g, w_in, ret_gn_g, w_ret_o, conv_w, conv_b, dt_bias_f,
           dt_bias_b, a_log_f, a_log_b, ssm_d, ssm_norm_g, w_ssm_o, w_out, norm_mlp_g, w_mlp_up,
           w_mlp_down, g_final):
    m = bsz * seq
    dt0 = COL_C + SSM_GROUPS * SSM_STATE
    w_main = jnp.concatenate([w_in[:, :dt0], w_in[:, dt0 + 2 * SSM_HEADS:]], axis=1).astype(BF16)
    wdt = w_in[:, dt0:dt0 + 2 * SSM_HEADS]
    wdt_f = wdt[:, :SSM_HEADS].reshape(D_MODEL, SSM_GROUPS, SSM_HPG)
    wdt_b = wdt[:, SSM_HEADS:].reshape(D_MODEL, SSM_GROUPS, SSM_HPG)
    wdt_pad = jnp.zeros((D_MODEL, SSM_GROUPS, LANES - 2 * SSM_HPG), F32)
    w_dt = jnp.concatenate([wdt_f, wdt_b, wdt_pad], axis=2).reshape(D_MODEL, SSM_GROUPS * LANES)
    w_dt = w_dt.astype(BF16)

    inv = (ROPE_BASE ** (-jnp.arange(ROPE_HALF, dtype=F32) / ROPE_HALF)).reshape(1, ROPE_HALF)
    tm1 = min(1024, m)
    proj, dt = _inproj(x2, pos2, inv, norm_mix_g.reshape(1, D_MODEL), w_main, w_dt, tm1)

    logg = jnp.log1p(-jnp.exp2(-5.0 - jnp.arange(RET_HEADS, dtype=F32)))
    yret = _retention(proj, logg, bsz, seq)

    nbc = SSM_GROUPS * SSM_STATE
    cw_x, cw_b, cw_c = (conv_w[:, :SSM_INNER], conv_w[:, SSM_INNER:SSM_INNER + nbc],
                        conv_w[:, SSM_INNER + nbc:])
    cb2 = conv_b.reshape(1, -1)
    cb_x, cb_b, cb_c = (cb2[:, :SSM_INNER], cb2[:, SSM_INNER:SSM_INNER + nbc],
                        cb2[:, SSM_INNER + nbc:])
    dskip = jnp.repeat(ssm_d, SSM_HEAD_DIM).reshape(1, SSM_INNER)
    yssm = _ssd(proj, dt, cw_x, cw_b, cw_c, cb_x, cb_b, cb_c,
                _group_rows(dt_bias_f, dt_bias_b), _group_rows(a_log_f, a_log_b),
                dskip, ssm_norm_g.reshape(1, SSM_INNER), bsz, seq)

    tm4 = min(512, m)
    x1 = _merge(yret, yssm, proj, x2, ret_gn_g.reshape(1, RET_V), w_ret_o.astype(BF16),
                w_ssm_o.astype(BF16), w_out.astype(BF16), tm4)
    return _mlp(x1, norm_mlp_g.reshape(1, D_MODEL), w_mlp_up.astype(BF16),
                w_mlp_down.astype(BF16), g_final, tm4)


def kernel(x, positions, norm_mix_g, w_in, ret_gn_g, w_ret_o, conv_w, conv_b, dt_bias_f, dt_bias_b,
           a_log_f, a_log_b, ssm_d, ssm_norm_g, w_ssm_o, w_out, norm_mlp_g, w_mlp_up, w_mlp_down,
           norm_final_g):
    bsz, seq, _ = x.shape
    depth = w_in.shape[0]
    assert depth == 1, "the final-norm fusion in the MLP call assumes a single layer"
    x2 = x.reshape(bsz * seq, D_MODEL)
    pos2 = jnp.broadcast_to(positions.reshape(bsz * seq, 1).astype(F32), (bsz * seq, ROPE_HALF))
    out = _layer(x2, pos2, bsz, seq, norm_mix_g[0], w_in[0], ret_gn_g[0], w_ret_o[0], conv_w[0],
                 conv_b[0], dt_bias_f[0], dt_bias_b[0], a_log_f[0], a_log_b[0], ssm_d[0],
                 ssm_norm_g[0], w_ssm_o[0], w_out[0], norm_mlp_g[0], w_mlp_up[0], w_mlp_down[0],
                 norm_final_g.reshape(1, D_MODEL))
    return out.reshape(bsz, seq, D_MODEL)
```
